```python
import jax, jax.numpy as jnp
from jax import lax
import numpy as np

D_MODEL = 1024
BATCH = 4
SEQ = 4096
DEPTH = 1

PLE_DIM = 256
N_HEADS = 8
N_KV = 2
HPG = N_HEADS // N_KV
HEAD_DIM = 64
ROT_DIM = HEAD_DIM // 4
ROPE_THETA = 500000.0
CMP_BLOCK = 32
CMP_STRIDE = 16
CMP_HIDDEN = 128
SEL_BLOCK = 64
SEL_TOPK = 16
WINDOW = 512
Q_BLOCK = 128
CONV_CH = 512
CONV_WIDTH = 31
FFN_DIM = 2816
FFN_CONV_WIDTH = 3
EPS = 1e-6
NEG = -1e30
FORCED_SCORE = 1e6

ATT_W = N_HEADS * HEAD_DIM
KV_W = N_KV * HEAD_DIM
IN_COLS = ATT_W + 6 * KV_W + 3 * N_HEADS + 2 * CONV_CH + 2 * D_MODEL

kernel_name = "hybrid_nsa_conformer_convffn_block"


def rms_norm(x, g):
    xf = x.astype(jnp.float32)
    y = xf * lax.rsqrt(jnp.mean(xf * xf, axis=-1, keepdims=True) + EPS)
    return (y * g.astype(jnp.float32)).astype(x.dtype)


def layer_norm(x, g, b):
    xf = x.astype(jnp.float32)
    mu = jnp.mean(xf, axis=-1, keepdims=True)
    var = jnp.mean(jnp.square(xf - mu), axis=-1, keepdims=True)
    y = (xf - mu) * lax.rsqrt(var + EPS)
    return (y * g.astype(jnp.float32) + b.astype(jnp.float32)).astype(x.dtype)


def rope_partial(x, pos):
    half = ROT_DIM // 2
    inv = ROPE_THETA ** (-jnp.arange(0, ROT_DIM, 2, dtype=jnp.float32) / ROT_DIM)
    ang = pos.astype(jnp.float32)[..., None] * inv
    ang = ang.reshape((pos.shape[0],) + (1,) * (x.ndim - 3) + (pos.shape[1], half))
    cos = jnp.cos(ang).astype(x.dtype)
    sin = jnp.sin(ang).astype(x.dtype)
    x1 = x[..., :half]
    x2 = x[..., half:ROT_DIM]
    return jnp.concatenate([x1 * cos - x2 * sin, x2 * cos + x1 * sin, x[..., ROT_DIM:]], axis=-1)


def causal_dwconv(x, w, b):
    k = w.shape[0]
    y = lax.conv_general_dilated(x, w.astype(x.dtype), window_strides=(1,), padding=((k - 1, 0),),
                                 dimension_numbers=('NWC', 'WIO', 'NWC'), feature_group_count=x.shape[-1])
    return y + b.astype(x.dtype)


def masked_softmax(s, mask):
    return jax.nn.softmax(jnp.where(mask, s.astype(jnp.float32), NEG), axis=-1)


def nsa_attention(q, kc, vc, ks, vs, kw, vw, gate_logits, positions, pe_k, pe_v, ck1, ck2, cv1, cv2):
    B, S = q.shape[0], q.shape[1]
    q = q.reshape(B, S, N_KV, HPG, HEAD_DIM).transpose(0, 2, 3, 1, 4)
    heads = lambda t: t.reshape(B, S, N_KV, HEAD_DIM).transpose(0, 2, 1, 3)
    kc, vc, ks, vs, kw, vw = heads(kc), heads(vc), heads(ks), heads(vs), heads(kw), heads(vw)
    q = rope_partial(q, positions)
    ks = rope_partial(ks, positions)
    kw = rope_partial(kw, positions)

    n_cmp = (S - CMP_BLOCK) // CMP_STRIDE + 1
    cmp_start = jnp.arange(n_cmp) * CMP_STRIDE
    cmp_idx = cmp_start[:, None] + jnp.arange(CMP_BLOCK)[None, :]
    cmp_end = cmp_start + CMP_BLOCK - 1

    def compress(t, pe, w1, w2):
        blk = (t[:, :, cmp_idx] + pe).reshape(B, N_KV, n_cmp, CMP_BLOCK * HEAD_DIM)
        return jax.nn.gelu(blk @ w1) @ w2

    k_cmp = rope_partial(compress(kc, pe_k, ck1, ck2), positions[:, cmp_end])
    v_cmp = compress(vc, pe_v, cv1, cv2)

    n_sel = S // SEL_BLOCK
    top_n = min(SEL_TOPK, n_sel)
    k_sel = ks.reshape(B, N_KV, n_sel, SEL_BLOCK, HEAD_DIM)
    v_sel = vs.reshape(B, N_KV, n_sel, SEL_BLOCK, HEAD_DIM)
    sel_start = jnp.arange(n_sel) * SEL_BLOCK
    overlap = jnp.clip(jnp.minimum(cmp_start[:, None] + CMP_BLOCK, sel_start[None, :] + SEL_BLOCK)
                       - jnp.maximum(cmp_start[:, None], sel_start[None, :]), 0).astype(jnp.float32) / CMP_STRIDE

    kw_pad = jnp.pad(kw, ((0, 0), (0, 0), (WINDOW, 0), (0, 0)))
    vw_pad = jnp.pad(vw, ((0, 0), (0, 0), (WINDOW, 0), (0, 0)))

    gates = jax.nn.sigmoid(gate_logits).reshape(B, S, 3, N_KV, HPG).transpose(2, 0, 3, 4, 1)
    scale = HEAD_DIM ** -0.5
    bi = jnp.arange(B)[:, None, None, None]
    gi = jnp.arange(N_KV)[None, :, None, None]
    j_sel = jnp.arange(n_sel)

    def chunk(c):
        start = c * Q_BLOCK
        qc = lax.dynamic_slice_in_dim(q, start, Q_BLOCK, axis=3) * scale
        gc = lax.dynamic_slice_in_dim(gates, start, Q_BLOCK, axis=4)
        t = start + jnp.arange(Q_BLOCK)

        s = jnp.einsum('bghqd,bgnd->bghqn', qc, k_cmp)
        valid = cmp_end[None, :] <= t[:, None]
        p_cmp = jnp.where(valid, masked_softmax(s, valid), 0.0)
        o_cmp = jnp.einsum('bghqn,bgnd->bghqd', p_cmp.astype(v_cmp.dtype), v_cmp)

        imp = jnp.einsum('bghqn,nj->bgqj', p_cmp, overlap)
        cur = t // SEL_BLOCK
        forced = (j_sel[None] == 0) | (j_sel[None] == cur[:, None]) | (j_sel[None] == cur[:, None] - 1)
        blk_valid = j_sel[None] * SEL_BLOCK <= t[:, None]
        score = jnp.where(blk_valid, jnp.where(forced, FORCED_SCORE, imp), -1.0)
        _, idx = lax.top_k(score, top_n)
        kg = k_sel[bi, gi, idx]
        vg = v_sel[bi, gi, idx]
        s = jnp.einsum('bghqd,bgqnld->bghqnl', qc, kg)
        tok = idx[..., None] * SEL_BLOCK + jnp.arange(SEL_BLOCK)
        tmask = (tok <= t[:, None, None])[:, :, None].reshape(B, N_KV, 1, Q_BLOCK, top_n * SEL_BLOCK)
        p = masked_softmax(s.reshape(B, N_KV, HPG, Q_BLOCK, top_n * SEL_BLOCK), tmask)
        p = p.reshape(B, N_KV, HPG, Q_BLOCK, top_n, SEL_BLOCK).astype(vg.dtype)
        o_sel = jnp.einsum('bghqnl,bgqnld->bghqd', p, vg)

        kwc = lax.dynamic_slice_in_dim(kw_pad, start, Q_BLOCK + WINDOW, axis=2)
        vwc = lax.dynamic_slice_in_dim(vw_pad, start, Q_BLOCK + WINDOW, axis=2)
        spos = start - WINDOW + jnp.arange(Q_BLOCK + WINDOW)
        wmask = (spos[None] >= 0) & (spos[None] <= t[:, None]) & (t[:, None] - spos[None] < WINDOW)
        s = jnp.einsum('bghqd,bgkd->bghqk', qc, kwc)
        p = masked_softmax(s, wmask).astype(vwc.dtype)
        o_win = jnp.einsum('bghqk,bgkd->bghqd', p, vwc)

        o = gc[0][..., None] * o_cmp + gc[1][..., None] * o_sel + gc[2][..., None] * o_win
        return o.transpose(0, 3, 1, 2, 4).reshape(B, Q_BLOCK, ATT_W)

    out = lax.map(chunk, jnp.arange(S // Q_BLOCK))
    return out.transpose(1, 0, 2, 3).reshape(B, S, ATT_W)


def conformer_conv(z, dw_w, dw_b, ln_g, ln_b):
    a, b = jnp.split(z, 2, axis=-1)
    y = causal_dwconv(a * jax.nn.sigmoid(b), dw_w, dw_b)
    return jax.nn.silu(layer_norm(y, ln_g, ln_b))


def setup_inputs(seed: int = 0) -> dict:
    key = jax.random.key(seed)
    ks = jax.random.split(key, 32)
    f32 = jnp.float32
    nrm = lambda k, shape, sc: jax.random.normal(k, shape, f32) * sc
    gain = lambda k, shape: 1.0 + 0.01 * jax.random.normal(k, shape, f32)
    L = DEPTH
    return {
        "x": jax.random.normal(ks[0], (BATCH, SEQ, D_MODEL), f32),
        "p": jax.random.normal(ks[1], (DEPTH, BATCH, SEQ, PLE_DIM), f32),
        "positions": (jnp.arange(SEQ, dtype=jnp.int32)[None, :]
                      + jax.random.randint(ks[2], (BATCH, 1), 0, 1024, dtype=jnp.int32)),
        "norm_mix_g": gain(ks[3], (L, D_MODEL)),
        "w_in": nrm(ks[4], (L, D_MODEL, IN_COLS), D_MODEL ** -0.5),
        "pe_k": nrm(ks[5], (L, CMP_BLOCK, HEAD_DIM), 0.5),
        "pe_v": nrm(ks[6], (L, CMP_BLOCK, HEAD_DIM), 0.5),
        "cmp_k_w1": nrm(ks[7], (L, CMP_BLOCK * HEAD_DIM, CMP_HIDDEN), (CMP_BLOCK * HEAD_DIM) ** -0.5),
        "cmp_k_w2": nrm(ks[8], (L, CMP_HIDDEN, HEAD_DIM), CMP_HIDDEN ** -0.5),
        "cmp_v_w1": nrm(ks[9], (L, CMP_BLOCK * HEAD_DIM, CMP_HIDDEN), (CMP_BLOCK * HEAD_DIM) ** -0.5),
        "cmp_v_w2": nrm(ks[10], (L, CMP_HIDDEN, HEAD_DIM), CMP_HIDDEN ** -0.5),
        "conv_dw_w": nrm(ks[11], (L, CONV_WIDTH, 1, CONV_CH), CONV_WIDTH ** -0.5),
        "conv_dw_b": nrm(ks[12], (L, CONV_CH), 0.01),
        "conv_ln_g": gain(ks[13], (L, CONV_CH)),
        "conv_ln_b": nrm(ks[14], (L, CONV_CH), 0.01),
        "w_a": nrm(ks[15], (L, ATT_W, D_MODEL), ATT_W ** -0.5),
        "w_b": nrm(ks[16], (L, CONV_CH, D_MODEL), CONV_CH ** -0.5),
        "w_o": nrm(ks[17], (L, D_MODEL, D_MODEL), D_MODEL ** -0.5),
        "norm_ffn_g": gain(ks[18], (L, D_MODEL)),
        "w_up": nrm(ks[19], (L, D_MODEL, 2 * FFN_DIM), D_MODEL ** -0.5),
        "ffn_dw_w": nrm(ks[20], (L, FFN_CONV_WIDTH, 1, 2 * FFN_DIM), FFN_CONV_WIDTH ** -0.5),
        "ffn_dw_b": nrm(ks[21], (L, 2 * FFN_DIM), 0.01),
        "w_down": nrm(ks[22], (L, FFN_DIM, D_MODEL), FFN_DIM ** -0.5),
        "norm_ple_g": gain(ks[23], (L, D_MODEL)),
        "w_ple_gate": nrm(ks[24], (L, D_MODEL, D_MODEL), D_MODEL ** -0.5),
        "w_ple_proj": nrm(ks[25], (L, PLE_DIM, D_MODEL), PLE_DIM ** -0.5),
        "norm_final_g": gain(ks[26], (D_MODEL,)),
    }


def reference(x, p, positions, norm_mix_g, w_in, pe_k, pe_v, cmp_k_w1, cmp_k_w2, cmp_v_w1, cmp_v_w2,
              conv_dw_w, conv_dw_b, conv_ln_g, conv_ln_b, w_a, w_b, w_o, norm_ffn_g, w_up, ffn_dw_w,
              ffn_dw_b, w_down, norm_ple_g, w_ple_gate, w_ple_proj, norm_final_g):
    sizes = [ATT_W] + [KV_W] * 6 + [3 * N_HEADS, 2 * CONV_CH, 2 * D_MODEL]
    offsets = np.cumsum(sizes)[:-1].tolist()
    h = x
    for i in range(DEPTH):
        u = rms_norm(h, norm_mix_g[i])
        z = u @ w_in[i]
        q, kc, vc, ks_, vs_, kw, vw, nsa_g, glu_in, merge_g = jnp.split(z, offsets, axis=-1)
        y_a = nsa_attention(q, kc, vc, ks_, vs_, kw, vw, nsa_g, positions, pe_k[i], pe_v[i],
                            cmp_k_w1[i], cmp_k_w2[i], cmp_v_w1[i], cmp_v_w2[i]) @ w_a[i]
        y_b = conformer_conv(glu_in, conv_dw_w[i], conv_dw_b[i], conv_ln_g[i], conv_ln_b[i]) @ w_b[i]
        g_a, g_b = jnp.split(merge_g, 2, axis=-1)
        h = h + (jax.nn.sigmoid(g_a) * y_a + jax.nn.sigmoid(g_b) * y_b) @ w_o[i]
        u = rms_norm(h, norm_ffn_g[i])
        up = causal_dwconv(u @ w_up[i], ffn_dw_w[i], ffn_dw_b[i])
        a, g = jnp.split(up, 2, axis=-1)
        h = h + (jax.nn.silu(g) * a) @ w_down[i]
        u = rms_norm(h, norm_ple_g[i])
        h = h + jax.nn.sigmoid(u @ w_ple_gate[i]) * (p[i] @ w_ple_proj[i])
    return rms_norm(h, norm_final_g)
```

```python
import functools

import numpy as np
import jax
import jax.numpy as jnp
from jax import lax
from jax.experimental import pallas as pl
from jax.experimental.pallas import tpu as pltpu

D_MODEL = 1024
PLE_DIM = 256
N_HEADS = 8
N_KV = 2
HPG = N_HEADS // N_KV
HEAD_DIM = 64
ROT_DIM = HEAD_DIM // 4
ROPE_THETA = 500000.0
CMP_BLOCK = 32
CMP_STRIDE = 16
CMP_HIDDEN = 128
SEL_BLOCK = 64
SEL_TOPK = 16
WINDOW = 512
CONV_CH = 512
CONV_WIDTH = 31
FFN_DIM = 2816
FFN_CONV_WIDTH = 3
EPS = 1e-6
NEG = -1e30
FORCED_SCORE = 1e6

ATT_W = N_HEADS * HEAD_DIM
KV_W = N_KV * HEAD_DIM
QKV_COLS = ATT_W + 6 * KV_W
GATE_COLS = 3 * N_HEADS

LANES = 128
SUBLANES = 8
VMEM_LIMIT_BYTES = 56 * 1024 * 1024

TM_PROJ = 512
TQ = 128
TK_SEL = 512
TS_MIX = 512
TS_FFN = 512
FFN_CHUNK = 256
CONV_HALO = 32
CONV_ROWS = 32

F32 = jnp.float32
BF16 = jnp.bfloat16


def _cparams(sem):
    return pltpu.CompilerParams(dimension_semantics=sem, vmem_limit_bytes=VMEM_LIMIT_BYTES)


def _const_spec(shape):
    nd = len(shape)
    return pl.BlockSpec(shape, lambda *_: (0,) * nd, pipeline_mode=pl.Buffered(1))


def _rms(x, g):
    ms = jnp.mean(x * x, axis=-1, keepdims=True)
    return (x * lax.rsqrt(ms + EPS)) * g


def _rope(v, cos, sin_signed):
    d = lax.broadcasted_iota(jnp.int32, v.shape, 1) & (HEAD_DIM - 1)
    half = ROT_DIM // 2
    sw = jnp.where(d < half, pltpu.roll(v, LANES - half, 1), pltpu.roll(v, half, 1))
    return v * cos + sw * sin_signed


def _rope_table_kernel(pos_ref, inv_ref, sgn_ref, cos_ref, sin_ref):
    ang = pos_ref[...].astype(F32) * inv_ref[...]
    cos_ref[...] = jnp.cos(ang)
    sin_ref[...] = jnp.sin(ang) * sgn_ref[...]


def _rope_tables(pos_col, inv_row, sgn_row):
    n = pos_col.shape[0]
    tb = 512
    return pl.pallas_call(
        _rope_table_kernel,
        grid=(n // tb,),
        in_specs=[pl.BlockSpec((tb, 1), lambda i: (i, 0)),
                  _const_spec((1, LANES)), _const_spec((1, LANES))],
        out_specs=[pl.BlockSpec((tb, LANES), lambda i: (i, 0))] * 2,
        out_shape=[jax.ShapeDtypeStruct((n, LANES), F32)] * 2,
        compiler_params=_cparams(("parallel",)),
        name="rope_tables",
    )(pos_col, inv_row, sgn_row)


def _inproj_kernel(x_ref, g_ref, w_ref, cos_ref, sin_ref,
                   qpad_ref, kc_ref, vc_ref, ks0_ref, ks1_ref, vs_ref, kw_ref, vw_ref, gate_ref,
                   *, seq):
    tm = x_ref.shape[0]
    u = _rms(x_ref[...], g_ref[...])
    z = jnp.dot(u.astype(BF16), w_ref[...], preferred_element_type=F32)
    cos = cos_ref[...]
    sin = sin_ref[...]
    lane = lax.broadcasted_iota(jnp.int32, (tm, LANES), 1)
    low = lane < HEAD_DIM

    scale = HEAD_DIM ** -0.5
    for c in range(ATT_W // LANES):
        qc = _rope(z[:, c * LANES:(c + 1) * LANES], cos, sin) * scale
        qsw = pltpu.roll(qc, HEAD_DIM, 1)
        for e in range(2):
            hh = 2 * c + e
            grp = hh // HPG
            src = qc if e == grp else qsw
            keep = low if grp == 0 else jnp.logical_not(low)
            qpad_ref[:, hh * LANES:(hh + 1) * LANES] = jnp.where(keep, src, 0.0).astype(BF16)

    o = ATT_W
    kc_ref[...] = z[:, o:o + LANES]
    vc_ref[...] = z[:, o + LANES:o + 2 * LANES]
    ks = _rope(z[:, o + 2 * LANES:o + 3 * LANES], cos, sin)
    vs_ref[...] = z[:, o + 3 * LANES:o + 4 * LANES].astype(BF16)
    kw_ref[...] = _rope(z[:, o + 4 * LANES:o + 5 * LANES], cos, sin).astype(BF16)
    vw_ref[...] = z[:, o + 5 * LANES:o + 6 * LANES].astype(BF16)
    gate_ref[...] = jax.nn.sigmoid(z[:, o + 6 * LANES:o + 7 * LANES])

    s0 = (pl.program_id(0) * tm) % seq
    blk = (s0 + lax.broadcasted_iota(jnp.int32, (tm, LANES), 0)) // SEL_BLOCK
    hot0 = (lane - HEAD_DIM == blk).astype(F32)
    hot1 = (lane == blk).astype(F32)
    ks0_ref[...] = jnp.where(low, ks, hot0).astype(BF16)
    ks1_ref[...] = jnp.where(low, hot1, ks).astype(BF16)


def _inproj(x2, g, w_qkv, cos_t, sin_t, seq):
    n = x2.shape[0]
    tm = TM_PROJ
    ncols = w_qkv.shape[1]
    tok = lambda w: pl.BlockSpec((tm, w), lambda i: (i, 0))
    outs = [(N_HEADS * LANES, BF16), (LANES, F32), (LANES, F32), (LANES, BF16), (LANES, BF16),
            (LANES, BF16), (LANES, BF16), (LANES, BF16), (LANES, F32)]
    return pl.pallas_call(
        functools.partial(_inproj_kernel, seq=seq),
        grid=(n // tm,),
        in_specs=[tok(D_MODEL), _const_spec((1, D_MODEL)), _const_spec((D_MODEL, ncols)),
                  tok(LANES), tok(LANES)],
        out_specs=[tok(w) for w, _ in outs],
        out_shape=[jax.ShapeDtypeStruct((n, w), dt) for w, dt in outs],
        compiler_params=_cparams(("parallel",)),
        name="in_proj",
    )(x2, g, w_qkv, cos_t, sin_t)


def _compress_kernel(kx_ref, vx_ref, pek_ref, pev_ref, wka_ref, wkb_ref, wk2_ref,
                     wva_ref, wvb_ref, wv2_ref, cos_ref, sin_ref, kcmp_ref, vcmp_ref):
    def comp(x, pe_ref, wa_ref, wb_ref, w2_ref):
        n = x.shape[0]
        xa = (x + pe_ref[0:1, :]).astype(BF16)
        xb = (x + pe_ref[1:2, :]).astype(BF16)
        ya = jnp.dot(xa, wa_ref[...], preferred_element_type=F32)
        yb = jnp.dot(xb, wb_ref[...], preferred_element_type=F32)
        h = ya + pltpu.roll(yb, n - 1, 0)
        return jnp.dot(jax.nn.gelu(h).astype(BF16), w2_ref[...], preferred_element_type=F32)

    kc = comp(kx_ref[0], pek_ref, wka_ref, wkb_ref, wk2_ref)
    kcmp_ref[0] = _rope(kc, cos_ref[0], sin_ref[0]).astype(BF16)
    vcmp_ref[0] = comp(vx_ref[0], pev_ref, wva_ref, wvb_ref, wv2_ref).astype(BF16)


def _compress(kx, vx, pek, pev, wka, wkb, wk2, wva, wvb, wv2, cos_c, sin_c):
    b, n, w = kx.shape
    per_b = lambda s: pl.BlockSpec((1,) + s, lambda i: (i, 0, 0))
    return pl.pallas_call(
        _compress_kernel,
        grid=(b,),
        in_specs=[per_b((n, w)), per_b((n, w)), _const_spec(pek.shape), _const_spec(pev.shape),
                  _const_spec(wka.shape), _const_spec(wkb.shape), _const_spec(wk2.shape),
                  _const_spec(wva.shape), _const_spec(wvb.shape), _const_spec(wv2.shape),
                  per_b((n, LANES)), per_b((n, LANES))],
        out_specs=[per_b((n, LANES))] * 2,
        out_shape=[jax.ShapeDtypeStruct((b, n, LANES), BF16)] * 2,
        compiler_params=_cparams(("parallel",)),
        name="kv_compress",
    )(kx, vx, pek, pev, wka, wkb, wk2, wva, wvb, wv2, cos_c, sin_c)


def _nt_dot(a, b):
    return lax.dot_general(a, b, (((1,), (1,)), ((), ())), preferred_element_type=F32)


def _softmax_step(s, v, m, l, acc):
    m_new = jnp.maximum(m, jnp.max(s, axis=-1, keepdims=True))
    alpha = jnp.exp(m - m_new)
    p = jnp.exp(s - m_new)
    l_new = alpha * l + jnp.sum(p, axis=-1, keepdims=True)
    acc_new = alpha * acc + jnp.dot(p.astype(BF16), v, preferred_element_type=F32)
    return m_new, l_new, acc_new


def _attn_kernel(q_ref, kcmp_ref, vcmp_ref, ks0_ref, ks1_ref, vs_ref, kw_ref, vw_ref,
                 gate_ref, ov0_ref, ov1_ref, out_ref, *, seq):
    tq = q_ref.shape[1]
    n_cmp = kcmp_ref.shape[1]
    n_sel = seq // SEL_BLOCK
    top_n = min(SEL_TOPK, n_sel)
    rows = HPG * tq
    q0 = pl.program_id(1) * tq
    t_col = q0 + lax.broadcasted_iota(jnp.int32, (tq, 1), 0)
    lane = lax.broadcasted_iota(jnp.int32, (tq, LANES), 1)
    low = lane < HEAD_DIM

    def per_head(mask, s, fill):
        s3 = s.reshape(HPG, tq, s.shape[-1])
        return jnp.where(mask[None], s3, fill).reshape(s.shape)

    cmp_end = lax.broadcasted_iota(jnp.int32, (1, n_cmp), 1) * CMP_STRIDE + (CMP_BLOCK - 1)
    cmp_valid = cmp_end <= t_col
    kcmp = kcmp_ref[0]
    vcmp = vcmp_ref[0]
    gates = gate_ref[0]

    for g in range(N_KV):
        in_grp = low if g == 0 else jnp.logical_not(low)
        qz = jnp.concatenate(
            [q_ref[0, :, (g * HPG + h) * LANES:(g * HPG + h + 1) * LANES] for h in range(HPG)], axis=0)

        s = per_head(cmp_valid, _nt_dot(qz, kcmp), NEG)
        m = jnp.max(s, axis=-1, keepdims=True)
        e = jnp.exp(s - m)
        p = per_head(cmp_valid, e / jnp.sum(e, axis=-1, keepdims=True), 0.0)
        o_cmp = jnp.dot(p.astype(BF16), vcmp, preferred_element_type=F32)
        p3 = p.reshape(HPG, tq, n_cmp)
        p_sum = p3[0]
        for h in range(1, HPG):
            p_sum = p_sum + p3[h]
        ov_ref = ov0_ref if g == 0 else ov1_ref
        imp = jnp.dot(p_sum, ov_ref[...], preferred_element_type=F32, precision=lax.Precision.HIGHEST)

        off = HEAD_DIM if g == 0 else 0
        jj = lane - off
        real = jnp.logical_not(in_grp) & (jj < n_sel)
        cur = t_col // SEL_BLOCK
        forced = (jj == 0) | (jj == cur) | (jj == cur - 1)
        blk_valid = jj * SEL_BLOCK <= t_col
        score = jnp.where(blk_valid, jnp.where(forced, FORCED_SCORE, imp), -1.0)
        rank = jnp.zeros((tq, LANES), jnp.int32)
        for i in range(n_sel):
            si = score[:, off + i:off + i + 1]
            ge = jnp.where(si >= score, 1, 0)
            gt = jnp.where(si > score, 1, 0)
            rank = rank + jnp.where(jj > i, ge, gt)
        bias = jnp.where(real & (rank >= top_n), NEG, 0.0).astype(BF16)
        bias4 = jnp.concatenate([bias] * HPG, axis=0)
        in_grp4 = jnp.concatenate([in_grp] * HPG, axis=0)
        qb = jnp.where(in_grp4, qz, bias4)

        ks_ref = ks0_ref if g == 0 else ks1_ref

        def sel_tile(start, carry, causal):
            k = ks_ref[0, pl.ds(start, TK_SEL), :]
            v = vs_ref[0, pl.ds(start, TK_SEL), :]
            s = _nt_dot(qb, k)
            if causal:
                kpos = start + lax.broadcasted_iota(jnp.int32, (1, TK_SEL), 1)
                s = per_head(kpos <= t_col, s, NEG)
            return _softmax_step(s, v, *carry)

        init = (jnp.full((rows, 1), NEG, F32), jnp.zeros((rows, 1), F32), jnp.zeros((rows, LANES), F32))
        n_full = q0 // TK_SEL
        carry = lax.fori_loop(
            0, n_full, lambda kt, c: sel_tile(pl.multiple_of(kt * TK_SEL, TK_SEL), c, False), init)
        _, l, acc = sel_tile(pl.multiple_of(n_full * TK_SEL, TK_SEL), carry, True)
        o_sel = acc / l

        kpos = q0 + lax.broadcasted_iota(jnp.int32, (1, tq), 1)
        s = per_head(kpos <= t_col, _nt_dot(qz, kw_ref[0, pl.ds(q0, tq), :]), NEG)
        carry = _softmax_step(s, vw_ref[0, pl.ds(q0, tq), :], *init)
        a0 = pl.multiple_of(jnp.maximum(q0 - WINDOW, 0), tq)
        kpos = a0 + lax.broadcasted_iota(jnp.int32, (1, WINDOW), 1)
        wmask = (kpos < q0) & (t_col - kpos < WINDOW)
        s = per_head(wmask, _nt_dot(qz, kw_ref[0, pl.ds(a0, WINDOW), :]), NEG)
        _, l, acc = _softmax_step(s, vw_ref[0, pl.ds(a0, WINDOW), :], *carry)
        o_win = acc / l

        heads = []
        for h in range(HPG):
            col = g * HPG + h
            r = slice(h * tq, (h + 1) * tq)
            heads.append(gates[:, col:col + 1] * o_cmp[r]
                         + gates[:, N_HEADS + col:N_HEADS + col + 1] * o_sel[r]
                         + gates[:, 2 * N_HEADS + col:2 * N_HEADS + col + 1] * o_win[r])
        for c in range(HPG // 2):
            a, b = heads[2 * c], heads[2 * c + 1]
            if g == 0:
                pair = jnp.where(low, a, pltpu.roll(b, HEAD_DIM, 1))
            else:
                pair = jnp.where(low, pltpu.roll(a, HEAD_DIM, 1), b)
            oc = g * (HPG // 2) + c
            out_ref[0, :, oc * LANES:(oc + 1) * LANES] = pair.astype(out_ref.dtype)


def _attention(qpad, kcmp, vcmp, ks0, ks1, vs, kw, vw, gates, ov0, ov1):
    b, seq, _ = qpad.shape
    n_cmp = kcmp.shape[1]
    q_tile = lambda w: pl.BlockSpec((1, TQ, w), lambda i, j: (i, j, 0))
    per_b = lambda r: pl.BlockSpec((1, r, LANES), lambda i, j: (i, 0, 0))
    return pl.pallas_call(
        functools.partial(_attn_kernel, seq=seq),
        grid=(b, seq // TQ),
        in_specs=[q_tile(N_HEADS * LANES), per_b(n_cmp), per_b(n_cmp),
                  per_b(seq), per_b(seq), per_b(seq), per_b(seq), per_b(seq),
                  q_tile(LANES), _const_spec(ov0.shape), _const_spec(ov1.shape)],
        out_specs=q_tile(ATT_W),
        out_shape=jax.ShapeDtypeStruct((b, seq, ATT_W), BF16),
        compiler_params=_cparams(("parallel", "parallel")),
        name="nsa_attention",
    )(qpad, kcmp, vcmp, ks0, ks1, vs, kw, vw, gates, ov0, ov1)


def _mix_kernel(x_ref, g_ref, wgm_ref, attn_ref, dww_ref, dwb_ref, lng_ref, lnb_ref,
                wa_ref, wb_ref, wo_ref, h_ref, cbuf, shifted, ybuf):
    ts = x_ref.shape[1]

    @pl.when(pl.program_id(1) == 0)
    def _():
        cbuf[0:CONV_HALO, :] = jnp.zeros((CONV_HALO, CONV_CH), F32)

    x = x_ref[0]
    u = _rms(x, g_ref[...])
    zz = jnp.dot(u.astype(BF16), wgm_ref[...], preferred_element_type=F32)
    cbuf[CONV_HALO:CONV_HALO + ts, :] = zz[:, :CONV_CH] * jax.nn.sigmoid(zz[:, CONV_CH:2 * CONV_CH])

    full = cbuf[...]
    for r in range(1, SUBLANES):
        shifted[r - 1] = pltpu.roll(full, CONV_HALO + ts - r, 0)

    def conv_rows(r, _):
        r0 = pl.multiple_of(r * CONV_ROWS, CONV_ROWS)
        acc = jnp.broadcast_to(dwb_ref[...], (CONV_ROWS, CONV_CH))
        for k in range(CONV_WIDTH):
            off = CONV_HALO - CONV_WIDTH + 1 + k
            base = pl.multiple_of(r0 + (off // SUBLANES) * SUBLANES, SUBLANES)
            if off % SUBLANES == 0:
                tap = cbuf[pl.ds(base, CONV_ROWS), :]
            else:
                tap = shifted[off % SUBLANES - 1, pl.ds(base, CONV_ROWS), :]
            acc = acc + dww_ref[k:k + 1, :] * tap
        ybuf[pl.ds(r0, CONV_ROWS), :] = acc
        return 0

    lax.fori_loop(0, ts // CONV_ROWS, conv_rows, 0)
    cbuf[0:CONV_HALO, :] = cbuf[ts:ts + CONV_HALO, :]

    y = ybuf[...]
    mu = jnp.mean(y, axis=-1, keepdims=True)
    var = jnp.mean(jnp.square(y - mu), axis=-1, keepdims=True)
    yn = (y - mu) * lax.rsqrt(var + EPS) * lng_ref[...] + lnb_ref[...]
    c = yn * jax.nn.sigmoid(yn)
    y_b = jnp.dot(c.astype(BF16), wb_ref[...], preferred_element_type=F32)
    y_a = jnp.dot(attn_ref[0], wa_ref[...], preferred_element_type=F32)
    o = 2 * CONV_CH
    mix = (jax.nn.sigmoid(zz[:, o:o + D_MODEL]) * y_a
           + jax.nn.sigmoid(zz[:, o + D_MODEL:o + 2 * D_MODEL]) * y_b)
    h_ref[0] = x + jnp.dot(mix.astype(BF16), wo_ref[...], preferred_element_type=F32)


def _mix(x, g, wgm, attn, dww, dwb, lng, lnb, wa, wb, wo):
    b, seq, _ = x.shape
    ts = TS_MIX
    tile = lambda w: pl.BlockSpec((1, ts, w), lambda i, j: (i, j, 0))
    consts = [g, wgm]
    consts2 = [dww, dwb, lng, lnb, wa, wb, wo]
    return pl.pallas_call(
        _mix_kernel,
        grid=(b, seq // ts),
        in_specs=[tile(D_MODEL)] + [_const_spec(a.shape) for a in consts] + [tile(ATT_W)]
                 + [_const_spec(a.shape) for a in consts2],
        out_specs=tile(D_MODEL),
        out_shape=jax.ShapeDtypeStruct((b, seq, D_MODEL), F32),
        scratch_shapes=[pltpu.VMEM((CONV_HALO + ts, CONV_CH), F32),
                        pltpu.VMEM((SUBLANES - 1, CONV_HALO + ts, CONV_CH), F32),
                        pltpu.VMEM((ts, CONV_CH), F32)],
        compiler_params=_cparams(("arbitrary", "arbitrary")),
        name="conv_merge",
    )(x, g, wgm, attn, dww, dwb, lng, lnb, wa, wb, wo)


def _ffn_kernel(h_ref, gf_ref, wup_ref, fw_ref, fb_ref, wdn_ref, p_ref, gp_ref, wpg_ref, wpp_ref,
                gfin_ref, out_ref, ubuf, carry, acc, *, final_norm):
    ts = h_ref.shape[1]
    n_chunks, _, cw = wup_ref.shape
    fc = cw // 2

    @pl.when(pl.program_id(1) == 0)
    def _():
        carry[...] = jnp.zeros_like(carry)

    h = h_ref[0]
    u = _rms(h, gf_ref[...]).astype(BF16)
    acc[...] = jnp.zeros_like(acc)

    def chunk(c, _):
        up = jnp.dot(u, wup_ref[c], preferred_element_type=F32)
        ubuf[0:SUBLANES, :] = carry[c]
        ubuf[SUBLANES:SUBLANES + ts, :] = up
        carry[c] = ubuf[ts:ts + SUBLANES, :]
        fw = fw_ref[c]
        conv = fb_ref[c]
        for k in range(FFN_CONV_WIDTH):
            sh = SUBLANES - (FFN_CONV_WIDTH - 1) + k
            conv = conv + fw[k:k + 1, :] * ubuf[sh:sh + ts, :]
        a = conv[:, :fc]
        gt = conv[:, fc:]
        act = gt * jax.nn.sigmoid(gt) * a
        acc[...] += jnp.dot(act.astype(BF16), wdn_ref[c], preferred_element_type=F32)
        return 0

    lax.fori_loop(0, n_chunks, chunk, 0)
    h2 = h + acc[...]
    u3 = _rms(h2, gp_ref[...]).astype(BF16)
    gate = jax.nn.sigmoid(jnp.dot(u3, wpg_ref[...], preferred_element_type=F32))
    proj = jnp.dot(p_ref[0].astype(BF16), wpp_ref[...], preferred_element_type=F32)
    h3 = h2 + gate * proj
    out_ref[0] = _rms(h3, gfin_ref[...]) if final_norm else h3


def _ffn(h, gf, wup, fw, fb, wdn, p, gp, wpg, wpp, gfin, final_norm):
    b, seq, _ = h.shape
    ts = TS_FFN
    n_chunks, _, cw = wup.shape
    tile = lambda w: pl.BlockSpec((1, ts, w), lambda i, j: (i, j, 0))
    single = lambda a: _const_spec(a.shape)
    return pl.pallas_call(
        functools.partial(_ffn_kernel, final_norm=final_norm),
        grid=(b, seq // ts),
        in_specs=[tile(D_MODEL), single(gf), single(wup), single(fw), single(fb), single(wdn),
                  tile(PLE_DIM), single(gp), single(wpg), single(wpp), single(gfin)],
        out_specs=tile(D_MODEL),
        out_shape=jax.ShapeDtypeStruct((b, seq, D_MODEL), F32),
        scratch_shapes=[pltpu.VMEM((SUBLANES + ts, cw), F32),
                        pltpu.VMEM((n_chunks, SUBLANES, cw), F32),
                        pltpu.VMEM((ts, D_MODEL), F32)],
        compiler_params=_cparams(("arbitrary", "arbitrary")),
        name="conv_ffn",
    )(h, gf, wup, fw, fb, wdn, p, gp, wpg, wpp, gfin)


def _expand_cmp_weights(w1, w2):
    half = CMP_BLOCK // 2
    w1r = w1.reshape(2, half, HEAD_DIM, CMP_HIDDEN)
    wab = jnp.zeros((2, half, N_KV, HEAD_DIM, N_KV, CMP_HIDDEN), w1.dtype)
    w2e = jnp.zeros((N_KV, CMP_HIDDEN, N_KV, HEAD_DIM), w2.dtype)
    for g in range(N_KV):
        wab = wab.at[:, :, g, :, g, :].set(w1r)
        w2e = w2e.at[g, :, g, :].set(w2)
    wab = wab.reshape(2, half * KV_W, N_KV * CMP_HIDDEN).astype(BF16)
    return wab[0], wab[1], w2e.reshape(N_KV * CMP_HIDDEN, KV_W).astype(BF16)


def _expand_pe(pe):
    half = CMP_BLOCK // 2
    per = jnp.broadcast_to(pe.reshape(2, half, 1, HEAD_DIM), (2, half, N_KV, HEAD_DIM))
    return per.reshape(2, half * KV_W)


def kernel(x, p, positions, norm_mix_g, w_in, pe_k, pe_v, cmp_k_w1, cmp_k_w2, cmp_v_w1, cmp_v_w2, conv_dw_w, conv_dw_b, conv_ln_g, conv_ln_b, w_a, w_b, w_o, norm_ffn_g, w_up, ffn_dw_w, ffn_dw_b, w_down, norm_ple_g, w_ple_gate, w_ple_proj, norm_final_g):
    b, seq, _ = x.shape
    depth = w_in.shape[0]
    n_tok = b * seq
    n_cmp = seq // CMP_STRIDE
    n_sel = seq // SEL_BLOCK
    assert seq % TS_MIX == 0 and seq % TK_SEL == 0 and seq >= WINDOW and n_sel <= HEAD_DIM
    assert FFN_DIM % FFN_CHUNK == 0

    inv = ROPE_THETA ** (-jnp.arange(0, ROT_DIM, 2, dtype=F32) / ROT_DIM)
    d = np.arange(LANES) % HEAD_DIM
    inv_row = jnp.where(d < ROT_DIM, inv[d % (ROT_DIM // 2)], 0.0).reshape(1, LANES)
    sgn_row = jnp.asarray(np.where(d < ROT_DIM // 2, -1.0, 1.0), F32).reshape(1, LANES)
    cmp_start = jnp.arange(n_cmp) * CMP_STRIDE
    sel_start = jnp.arange(n_sel) * SEL_BLOCK
    overlap = jnp.clip(jnp.minimum(cmp_start[:, None] + CMP_BLOCK, sel_start[None, :] + SEL_BLOCK)
                       - jnp.maximum(cmp_start[:, None], sel_start[None, :]), 0).astype(F32) / CMP_STRIDE
    ov1 = jnp.pad(overlap, ((0, 0), (0, LANES - n_sel)))
    ov0 = jnp.pad(overlap, ((0, 0), (HEAD_DIM, LANES - HEAD_DIM - n_sel)))

    cmp_end = np.minimum(np.arange(n_cmp) * CMP_STRIDE + CMP_BLOCK - 1, seq - 1)
    pos_all = jnp.concatenate([positions.reshape(-1), positions[:, cmp_end].reshape(-1)])
    pad = (-pos_all.shape[0]) % 512
    pos_all = jnp.pad(pos_all, (0, pad)).reshape(-1, 1)
    cos_t, sin_t = _rope_tables(pos_all, inv_row, sgn_row)
    cos_c = cos_t[n_tok:n_tok + b * n_cmp].reshape(b, n_cmp, LANES)
    sin_c = sin_t[n_tok:n_tok + b * n_cmp].reshape(b, n_cmp, LANES)

    h = x
    row = lambda v: v.reshape(1, -1)
    n_chunks = FFN_DIM // FFN_CHUNK
    for i in range(depth):
        w_qkv = jnp.concatenate(
            [w_in[i][:, :QKV_COLS + GATE_COLS],
             jnp.zeros((D_MODEL, LANES - GATE_COLS), w_in.dtype)], axis=1).astype(BF16)
        w_gm = w_in[i][:, QKV_COLS + GATE_COLS:].astype(BF16)

        qpad, kc, vc, ks0, ks1, vs, kw, vw, gates = _inproj(
            h.reshape(n_tok, D_MODEL), row(norm_mix_g[i]), w_qkv, cos_t, sin_t, seq)

        wka, wkb, wk2 = _expand_cmp_weights(cmp_k_w1[i], cmp_k_w2[i])
        wva, wvb, wv2 = _expand_cmp_weights(cmp_v_w1[i], cmp_v_w2[i])
        kcmp, vcmp = _compress(kc.reshape(b, n_cmp, CMP_STRIDE * KV_W), vc.reshape(b, n_cmp, CMP_STRIDE * KV_W),
                               _expand_pe(pe_k[i]), _expand_pe(pe_v[i]),
                               wka, wkb, wk2, wva, wvb, wv2, cos_c, sin_c)

        s3 = lambda a: a.reshape(b, seq, a.shape[-1])
        attn = _attention(s3(qpad), kcmp, vcmp, s3(ks0), s3(ks1), s3(vs), s3(kw), s3(vw), s3(gates), ov0, ov1)

        h = _mix(h, row(norm_mix_g[i]), w_gm, attn,
                 conv_dw_w[i].reshape(CONV_WIDTH, CONV_CH), row(conv_dw_b[i]),
                 row(conv_ln_g[i]), row(conv_ln_b[i]),
                 w_a[i].astype(BF16), w_b[i].astype(BF16), w_o[i].astype(BF16))

        def regroup(a):
            lead = a.shape[:-1]
            a2 = a.reshape(lead + (2, n_chunks, FFN_CHUNK))
            a2 = jnp.moveaxis(a2, -3, -2).reshape(lead + (n_chunks, 2 * FFN_CHUNK))
            return jnp.moveaxis(a2, -2, 0)
        wup = regroup(w_up[i]).astype(BF16)
        fw = regroup(ffn_dw_w[i].reshape(FFN_CONV_WIDTH, 2 * FFN_DIM))
        fb = regroup(ffn_dw_b[i].reshape(1, 2 * FFN_DIM))
        wdn = w_down[i].reshape(n_chunks, FFN_CHUNK, D_MODEL).astype(BF16)
        h = _ffn(h, row(norm_ffn_g[i]), wup, fw, fb, wdn, p[i], row(norm_ple_g[i]),
                 w_ple_gate[i].astype(BF16), w_ple_proj[i].astype(BF16), row(norm_final_g),
                 final_norm=(i == depth - 1))
    return h
```

```python
import functools

import numpy as np
import jax
import jax.numpy as jnp
from jax import lax
from jax.experimental import pallas as pl
from jax.experimental.pallas import tpu as pltpu

D_MODEL = 1024
PLE_DIM = 256
N_HEADS = 8
N_KV = 2
HPG = N_HEADS // N_KV
HEAD_DIM = 64
ROT_DIM = HEAD_DIM // 4
ROPE_THETA = 500000.0
CMP_BLOCK = 32
CMP_STRIDE = 16
CMP_HIDDEN = 128
SEL_BLOCK = 64
SEL_TOPK = 16
WINDOW = 512
CONV_CH = 512
CONV_WIDTH = 31
FFN_DIM = 2816
FFN_CONV_WIDTH = 3
EPS = 1e-6
NEG = -1e30
FORCED_SCORE = 1e6

ATT_W = N_HEADS * HEAD_DIM
KV_W = N_KV * HEAD_DIM
QKV_COLS = ATT_W + 6 * KV_W
GATE_COLS = 3 * N_HEADS

LANES = 128
SUBLANES = 8
VMEM_LIMIT_BYTES = 56 * 1024 * 1024

TM_PROJ = 512
TQ = 128
TK_SEL = 512
TS_MIX = 512
TS_FFN = 512
FFN_CHUNK = 256
CONV_HALO = 32
CONV_ROWS = 32

F32 = jnp.float32
BF16 = jnp.bfloat16


def _cparams(sem):
    return pltpu.CompilerParams(dimension_semantics=sem, vmem_limit_bytes=VMEM_LIMIT_BYTES)


def _const_spec(shape):
    nd = len(shape)
    return pl.BlockSpec(shape, lambda *_: (0,) * nd, pipeline_mode=pl.Buffered(1))


def _rms(x, g):
    ms = jnp.mean(x * x, axis=-1, keepdims=True)
    return (x * lax.rsqrt(ms + EPS)) * g


def _rope(v, cos, sin_signed):
    d = lax.broadcasted_iota(jnp.int32, v.shape, 1) & (HEAD_DIM - 1)
    half = ROT_DIM // 2
    sw = jnp.where(d < half, pltpu.roll(v, LANES - half, 1), pltpu.roll(v, half, 1))
    return v * cos + sw * sin_signed


def _rope_table_kernel(pos_ref, inv_ref, sgn_ref, cos_ref, sin_ref):
    ang = pos_ref[...].astype(F32) * inv_ref[...]
    cos_ref[...] = jnp.cos(ang)
    sin_ref[...] = jnp.sin(ang) * sgn_ref[...]


def _rope_tables(pos_col, inv_row, sgn_row):
    n = pos_col.shape[0]
    tb = 512
    return pl.pallas_call(
        _rope_table_kernel,
        grid=(n // tb,),
        in_specs=[pl.BlockSpec((tb, 1), lambda i: (i, 0)),
                  _const_spec((1, LANES)), _const_spec((1, LANES))],
        out_specs=[pl.BlockSpec((tb, LANES), lambda i: (i, 0))] * 2,
        out_shape=[jax.ShapeDtypeStruct((n, LANES), F32)] * 2,
        compiler_params=_cparams(("parallel",)),
        name="rope_tables",
    )(pos_col, inv_row, sgn_row)


def _inproj_kernel(x_ref, g_ref, w_ref, cos_ref, sin_ref,
                   qpad_ref, kc_ref, vc_ref, ks0_ref, ks1_ref, vs0_ref, vs1_ref, kw_ref, vw0_ref, vw1_ref,
                   gate_ref, *, seq):
    tm = x_ref.shape[0]
    u = _rms(x_ref[...], g_ref[...])
    z = jnp.dot(u.astype(BF16), w_ref[...], preferred_element_type=F32)
    cos = cos_ref[...]
    sin = sin_ref[...]
    lane = lax.broadcasted_iota(jnp.int32, (tm, LANES), 1)
    low = lane < HEAD_DIM

    scale = HEAD_DIM ** -0.5
    for c in range(ATT_W // LANES):
        qc = _rope(z[:, c * LANES:(c + 1) * LANES], cos, sin) * scale
        qsw = pltpu.roll(qc, HEAD_DIM, 1)
        for e in range(2):
            hh = 2 * c + e
            grp = hh // HPG
            src = qc if e == grp else qsw
            keep = low if grp == 0 else jnp.logical_not(low)
            qpad_ref[:, hh * LANES:(hh + 1) * LANES] = jnp.where(keep, src, 0.0).astype(BF16)

    o = ATT_W
    kc_ref[...] = z[:, o:o + LANES]
    vc_ref[...] = z[:, o + LANES:o + 2 * LANES]
    ks = _rope(z[:, o + 2 * LANES:o + 3 * LANES], cos, sin)
    kw_ref[...] = _rope(z[:, o + 4 * LANES:o + 5 * LANES], cos, sin).astype(BF16)
    gate_ref[...] = jax.nn.sigmoid(z[:, o + 6 * LANES:o + 7 * LANES])
    for v, (r0, r1) in ((z[:, o + 3 * LANES:o + 4 * LANES], (vs0_ref, vs1_ref)),
                        (z[:, o + 5 * LANES:o + 6 * LANES], (vw0_ref, vw1_ref))):
        r0[...] = jnp.where(low, v, 1.0).astype(BF16)
        r1[...] = jnp.where(low, 1.0, v).astype(BF16)

    s0 = (pl.program_id(0) * tm) % seq
    blk = (s0 + lax.broadcasted_iota(jnp.int32, (tm, LANES), 0)) // SEL_BLOCK
    hot0 = (lane - HEAD_DIM == blk).astype(F32)
    hot1 = (lane == blk).astype(F32)
    ks0_ref[...] = jnp.where(low, ks, hot0).astype(BF16)
    ks1_ref[...] = jnp.where(low, hot1, ks).astype(BF16)


def _inproj(x2, g, w_qkv, cos_t, sin_t, seq):
    n = x2.shape[0]
    tm = TM_PROJ
    ncols = w_qkv.shape[1]
    tok = lambda w: pl.BlockSpec((tm, w), lambda i: (i, 0))
    outs = [(N_HEADS * LANES, BF16), (LANES, F32), (LANES, F32)] + [(LANES, BF16)] * 7 + [(LANES, F32)]
    return pl.pallas_call(
        functools.partial(_inproj_kernel, seq=seq),
        grid=(n // tm,),
        in_specs=[tok(D_MODEL), _const_spec((1, D_MODEL)), _const_spec((D_MODEL, ncols)),
                  tok(LANES), tok(LANES)],
        out_specs=[tok(w) for w, _ in outs],
        out_shape=[jax.ShapeDtypeStruct((n, w), dt) for w, dt in outs],
        compiler_params=_cparams(("parallel",)),
        name="in_proj",
    )(x2, g, w_qkv, cos_t, sin_t)


def _compress_kernel(kx_ref, vx_ref, pek_ref, pev_ref, wka_ref, wkb_ref, wk2_ref,
                     wva_ref, wvb_ref, wv2_ref, cos_ref, sin_ref, kcmp_ref, vcmp_ref):
    def comp(x, pe_ref, wa_ref, wb_ref, w2_ref):
        n = x.shape[0]
        xa = (x + pe_ref[0:1, :]).astype(BF16)
        xb = (x + pe_ref[1:2, :]).astype(BF16)
        ya = jnp.dot(xa, wa_ref[...], preferred_element_type=F32)
        yb = jnp.dot(xb, wb_ref[...], preferred_element_type=F32)
        h = ya + pltpu.roll(yb, n - 1, 0)
        return jnp.dot(jax.nn.gelu(h).astype(BF16), w2_ref[...], preferred_element_type=F32)

    kc = comp(kx_ref[0], pek_ref, wka_ref, wkb_ref, wk2_ref)
    kcmp_ref[0] = _rope(kc, cos_ref[0], sin_ref[0]).astype(BF16)
    vcmp_ref[0] = comp(vx_ref[0], pev_ref, wva_ref, wvb_ref, wv2_ref).astype(BF16)


def _compress(kx, vx, pek, pev, wka, wkb, wk2, wva, wvb, wv2, cos_c, sin_c):
    b, n, w = kx.shape
    per_b = lambda s: pl.BlockSpec((1,) + s, lambda i: (i, 0, 0))
    return pl.pallas_call(
        _compress_kernel,
        grid=(b,),
        in_specs=[per_b((n, w)), per_b((n, w)), _const_spec(pek.shape), _const_spec(pev.shape),
                  _const_spec(wka.shape), _const_spec(wkb.shape), _const_spec(wk2.shape),
                  _const_spec(wva.shape), _const_spec(wvb.shape), _const_spec(wv2.shape),
                  per_b((n, LANES)), per_b((n, LANES))],
        out_specs=[per_b((n, LANES))] * 2,
        out_shape=[jax.ShapeDtypeStruct((b, n, LANES), BF16)] * 2,
        compiler_params=_cparams(("parallel",)),
        name="kv_compress",
    )(kx, vx, pek, pev, wka, wkb, wk2, wva, wvb, wv2, cos_c, sin_c)


def _nt_dot(a, b):
    return lax.dot_general(a, b, (((1,), (1,)), ((), ())), preferred_element_type=F32)


def _softmax_step(s, v, m, acc):
    m_new = jnp.maximum(m, jnp.max(s, axis=-1, keepdims=True))
    p = jnp.exp(s - m_new)
    acc_new = jnp.exp(m - m_new) * acc + jnp.dot(p.astype(BF16), v, preferred_element_type=F32)
    return m_new, acc_new


def _attn_kernel(q_ref, kcmp_ref, vcmp_ref, ks0_ref, ks1_ref, vs0_ref, vs1_ref, kw_ref, vw0_ref, vw1_ref,
                 gate_ref, ovt_ref, out_ref, *, seq):
    tq = q_ref.shape[1]
    n_cmp = kcmp_ref.shape[1]
    n_sel = seq // SEL_BLOCK
    top_n = min(SEL_TOPK, n_sel)
    rows = HPG * tq
    groups = range(N_KV)
    q0 = pl.multiple_of(pl.program_id(1) * tq, tq)
    t_col = q0 + lax.broadcasted_iota(jnp.int32, (tq, 1), 0)
    t_row = q0 + lax.broadcasted_iota(jnp.int32, (1, tq), 1)
    low = lax.broadcasted_iota(jnp.int32, (tq, LANES), 1) < HEAD_DIM
    low4 = lax.broadcasted_iota(jnp.int32, (rows, LANES), 1) < HEAD_DIM
    in_grp4 = (low4, jnp.logical_not(low4))
    ks_refs, vs_refs, vw_refs = (ks0_ref, ks1_ref), (vs0_ref, vs1_ref), (vw0_ref, vw1_ref)

    def per_head(mask, s, fill):
        s3 = s.reshape(HPG, tq, s.shape[-1])
        return jnp.where(mask[None], s3, fill).reshape(s.shape)

    cmp_end = lax.broadcasted_iota(jnp.int32, (1, n_cmp), 1) * CMP_STRIDE + (CMP_BLOCK - 1)
    cmp_valid = cmp_end <= t_col
    kcmp = kcmp_ref[0]
    vcmp = vcmp_ref[0]
    gates = gate_ref[0]
    qz = [jnp.concatenate(
        [q_ref[0, :, (g * HPG + h) * LANES:(g * HPG + h + 1) * LANES] for h in range(HPG)], axis=0)
        for g in groups]

    sub = lax.broadcasted_iota(jnp.int32, (SUBLANES, tq), 0)
    cur = t_row // SEL_BLOCK
    n_blk_rows = n_sel // SUBLANES
    o_cmp, bias_t = [], []
    for g in groups:
        s = per_head(cmp_valid, _nt_dot(qz[g], kcmp), NEG)
        m = jnp.max(s, axis=-1, keepdims=True)
        e = jnp.exp(s - m)
        p = per_head(cmp_valid, e / jnp.sum(e, axis=-1, keepdims=True), 0.0)
        o_cmp.append(jnp.dot(p.astype(BF16), vcmp, preferred_element_type=F32))
        p3 = p.reshape(HPG, tq, n_cmp)
        p_sum = p3[0]
        for h in range(1, HPG):
            p_sum = p_sum + p3[h]
        imp_t = lax.dot_general(ovt_ref[...], p_sum, (((1,), (1,)), ((), ())),
                                preferred_element_type=F32, precision=lax.Precision.HIGHEST)

        score = []
        for v in range(n_blk_rows):
            jj = sub + v * SUBLANES
            forced = (jj == 0) | (jj == cur) | (jj == cur - 1)
            blk_valid = jj * SEL_BLOCK <= t_row
            score.append(jnp.where(blk_valid,
                                   jnp.where(forced, FORCED_SCORE, imp_t[v * SUBLANES:(v + 1) * SUBLANES]), -1.0))
        rank = [jnp.zeros((SUBLANES, tq), jnp.int32)] * n_blk_rows
        for i in range(n_sel):
            vi, ri = divmod(i, SUBLANES)
            si = score[vi][ri:ri + 1, :]
            for v in range(n_blk_rows):
                ge = jnp.where(si >= score[v], 1, 0)
                gt = jnp.where(si > score[v], 1, 0)
                rank[v] = rank[v] + (ge if v > vi else gt if v < vi else jnp.where(sub > ri, ge, gt))
        blocks = [jnp.where(r >= top_n, NEG, 0.0) for r in rank]
        blocks += [jnp.zeros((SUBLANES, tq), F32)] * (HEAD_DIM // SUBLANES - n_blk_rows)
        bias_t.append(jnp.concatenate(blocks, axis=0))

    bias = jnp.concatenate([bias_t[1], bias_t[0]], axis=0).T.astype(BF16)
    bias4 = jnp.concatenate([bias] * HPG, axis=0)
    qb = [jnp.where(in_grp4[g], qz[g], bias4) for g in groups]

    def normalize(acc, g):
        den = HEAD_DIM if g == 0 else 0
        return acc / acc[:, den:den + 1]

    def sel_tile(start, carry, causal):
        out = []
        for g in groups:
            s = _nt_dot(qb[g], ks_refs[g][0, pl.ds(start, TK_SEL), :])
            if causal:
                kpos = start + lax.broadcasted_iota(jnp.int32, (1, TK_SEL), 1)
                s = per_head(kpos <= t_col, s, NEG)
            out.extend(_softmax_step(s, vs_refs[g][0, pl.ds(start, TK_SEL), :], carry[2 * g], carry[2 * g + 1]))
        return tuple(out)

    init = (jnp.full((rows, 1), NEG, F32), jnp.zeros((rows, LANES), F32))
    n_full = q0 // TK_SEL

    def sel_pair(kp, c):
        start = pl.multiple_of(kp * (2 * TK_SEL), 2 * TK_SEL)
        return sel_tile(start + TK_SEL, sel_tile(start, c, False), False)

    carry = lax.fori_loop(0, n_full // 2, sel_pair, init * N_KV)
    carry = lax.fori_loop(
        (n_full // 2) * 2, n_full, lambda kt, c: sel_tile(pl.multiple_of(kt * TK_SEL, TK_SEL), c, False), carry)
    carry = sel_tile(pl.multiple_of(n_full * TK_SEL, TK_SEL), carry, True)
    o_sel = [normalize(carry[2 * g + 1], g) for g in groups]

    a0 = pl.multiple_of(jnp.maximum(q0 - WINDOW, 0), tq)
    dmask = q0 + lax.broadcasted_iota(jnp.int32, (1, tq), 1) <= t_col
    kpos = a0 + lax.broadcasted_iota(jnp.int32, (1, WINDOW), 1)
    wmask = (kpos < q0) & (t_col - kpos < WINDOW)
    o_win = []
    for g in groups:
        s = per_head(dmask, _nt_dot(qz[g], kw_ref[0, pl.ds(q0, tq), :]), NEG)
        c = _softmax_step(s, vw_refs[g][0, pl.ds(q0, tq), :], *init)
        s = per_head(wmask, _nt_dot(qz[g], kw_ref[0, pl.ds(a0, WINDOW), :]), NEG)
        o_win.append(normalize(_softmax_step(s, vw_refs[g][0, pl.ds(a0, WINDOW), :], *c)[1], g))

    for g in groups:
        heads = []
        for h in range(HPG):
            col = g * HPG + h
            r = slice(h * tq, (h + 1) * tq)
            heads.append(gates[:, col:col + 1] * o_cmp[g][r]
                         + gates[:, N_HEADS + col:N_HEADS + col + 1] * o_sel[g][r]
                         + gates[:, 2 * N_HEADS + col:2 * N_HEADS + col + 1] * o_win[g][r])
        for c in range(HPG // 2):
            a, b = heads[2 * c], heads[2 * c + 1]
            if g == 0:
                pair = jnp.where(low, a, pltpu.roll(b, HEAD_DIM, 1))
            else:
                pair = jnp.where(low, pltpu.roll(a, HEAD_DIM, 1), b)
            oc = g * (HPG // 2) + c
            out_ref[0, :, oc * LANES:(oc + 1) * LANES] = pair.astype(out_ref.dtype)


def _attention(qpad, kcmp, vcmp, seq_kv, gates, ov_t):
    b, seq, _ = qpad.shape
    n_cmp = kcmp.shape[1]
    q_tile = lambda w: pl.BlockSpec((1, TQ, w), lambda i, j: (i, j, 0))
    per_b = lambda r: pl.BlockSpec((1, r, LANES), lambda i, j: (i, 0, 0))
    return pl.pallas_call(
        functools.partial(_attn_kernel, seq=seq),
        grid=(b, seq // TQ),
        in_specs=[q_tile(N_HEADS * LANES), per_b(n_cmp), per_b(n_cmp)] + [per_b(seq)] * len(seq_kv)
                 + [q_tile(LANES), _const_spec(ov_t.shape)],
        out_specs=q_tile(ATT_W),
        out_shape=jax.ShapeDtypeStruct((b, seq, ATT_W), BF16),
        compiler_params=_cparams(("parallel", "parallel")),
        name="nsa_attention",
    )(qpad, kcmp, vcmp, *seq_kv, gates, ov_t)


def _mix_kernel(x_ref, g_ref, wgm_ref, attn_ref, dww_ref, dwb_ref, lng_ref, lnb_ref,
                wa_ref, wb_ref, wo_ref, h_ref, cbuf, shifted, ybuf):
    ts = x_ref.shape[1]

    @pl.when(pl.program_id(1) == 0)
    def _():
        cbuf[0:CONV_HALO, :] = jnp.zeros((CONV_HALO, CONV_CH), F32)

    x = x_ref[0]
    u = _rms(x, g_ref[...]).astype(BF16)
    o = 2 * CONV_CH
    zg = jnp.dot(u, wgm_ref[:, :o], preferred_element_type=F32)
    cbuf[CONV_HALO:CONV_HALO + ts, :] = zg[:, :CONV_CH] * jax.nn.sigmoid(zg[:, CONV_CH:])
    zm = jnp.dot(u, wgm_ref[:, o:], preferred_element_type=F32)
    y_a = jnp.dot(attn_ref[0], wa_ref[...], preferred_element_type=F32)

    full = cbuf[...]
    for r in range(1, SUBLANES):
        shifted[r - 1] = pltpu.roll(full, CONV_HALO + ts - r, 0)

    for r0 in range(0, ts, CONV_ROWS):
        acc = jnp.broadcast_to(dwb_ref[...], (CONV_ROWS, CONV_CH))
        for k in range(CONV_WIDTH):
            off = CONV_HALO - CONV_WIDTH + 1 + k
            base = r0 + (off // SUBLANES) * SUBLANES
            if off % SUBLANES == 0:
                tap = cbuf[base:base + CONV_ROWS, :]
            else:
                tap = shifted[off % SUBLANES - 1, base:base + CONV_ROWS, :]
            acc = acc + dww_ref[k:k + 1, :] * tap
        ybuf[r0:r0 + CONV_ROWS, :] = acc
    cbuf[0:CONV_HALO, :] = cbuf[ts:ts + CONV_HALO, :]

    y = ybuf[...]
    mu = jnp.mean(y, axis=-1, keepdims=True)
    var = jnp.mean(jnp.square(y - mu), axis=-1, keepdims=True)
    yn = (y - mu) * lax.rsqrt(var + EPS) * lng_ref[...] + lnb_ref[...]
    c = yn * jax.nn.sigmoid(yn)
    y_b = jnp.dot(c.astype(BF16), wb_ref[...], preferred_element_type=F32)
    mix = jax.nn.sigmoid(zm[:, :D_MODEL]) * y_a + jax.nn.sigmoid(zm[:, D_MODEL:]) * y_b
    h_ref[0] = x + jnp.dot(mix.astype(BF16), wo_ref[...], preferred_element_type=F32)


def _mix(x, g, wgm, attn, dww, dwb, lng, lnb, wa, wb, wo):
    b, seq, _ = x.shape
    ts = TS_MIX
    tile = lambda w: pl.BlockSpec((1, ts, w), lambda i, j: (i, j, 0))
    consts = [g, wgm]
    consts2 = [dww, dwb, lng, lnb, wa, wb, wo]
    return pl.pallas_call(
        _mix_kernel,
        grid=(b, seq // ts),
        in_specs=[tile(D_MODEL)] + [_const_spec(a.shape) for a in consts] + [tile(ATT_W)]
                 + [_const_spec(a.shape) for a in consts2],
        out_specs=tile(D_MODEL),
        out_shape=jax.ShapeDtypeStruct((b, seq, D_MODEL), F32),
        scratch_shapes=[pltpu.VMEM((CONV_HALO + ts, CONV_CH), F32),
                        pltpu.VMEM((SUBLANES - 1, CONV_HALO + ts, CONV_CH), F32),
                        pltpu.VMEM((ts, CONV_CH), F32)],
        compiler_params=_cparams(("arbitrary", "arbitrary")),
        name="conv_merge",
    )(x, g, wgm, attn, dww, dwb, lng, lnb, wa, wb, wo)


def _ffn_kernel(h_ref, gf_ref, wup_ref, fw_ref, fb_ref, wdn_ref, p_ref, gp_ref, wpg_ref, wpp_ref,
                gfin_ref, out_ref, ubuf, carry, *, final_norm):
    ts = h_ref.shape[1]
    n_chunks, _, cw = wup_ref.shape
    fc = cw // 2

    @pl.when(pl.program_id(1) == 0)
    def _():
        carry[...] = jnp.zeros_like(carry)

    h = h_ref[0]
    u = _rms(h, gf_ref[...]).astype(BF16)

    h2 = h
    for c in range(n_chunks):
        buf = ubuf.at[c % 2]
        up = jnp.dot(u, wup_ref[c], preferred_element_type=F32)
        buf[0:SUBLANES, :] = carry[c]
        buf[SUBLANES:SUBLANES + ts, :] = up
        carry[c] = buf[ts:ts + SUBLANES, :]
        fw = fw_ref[c]
        conv = fb_ref[c]
        for k in range(FFN_CONV_WIDTH):
            sh = SUBLANES - (FFN_CONV_WIDTH - 1) + k
            conv = conv + fw[k:k + 1, :] * buf[sh:sh + ts, :]
        a = conv[:, :fc]
        gt = conv[:, fc:]
        act = gt * jax.nn.sigmoid(gt) * a
        h2 = h2 + jnp.dot(act.astype(BF16), wdn_ref[c], preferred_element_type=F32)

    u3 = _rms(h2, gp_ref[...]).astype(BF16)
    gate = jax.nn.sigmoid(jnp.dot(u3, wpg_ref[...], preferred_element_type=F32))
    proj = jnp.dot(p_ref[0].astype(BF16), wpp_ref[...], preferred_element_type=F32)
    h3 = h2 + gate * proj
    out_ref[0] = _rms(h3, gfin_ref[...]) if final_norm else h3


def _ffn(h, gf, wup, fw, fb, wdn, p, gp, wpg, wpp, gfin, final_norm):
    b, seq, _ = h.shape
    ts = TS_FFN
    n_chunks, _, cw = wup.shape
    tile = lambda w: pl.BlockSpec((1, ts, w), lambda i, j: (i, j, 0))
    single = lambda a: _const_spec(a.shape)
    return pl.pallas_call(
        functools.partial(_ffn_kernel, final_norm=final_norm),
        grid=(b, seq // ts),
        in_specs=[tile(D_MODEL), single(gf), single(wup), single(fw), single(fb), single(wdn),
                  tile(PLE_DIM), single(gp), single(wpg), single(wpp), single(gfin)],
        out_specs=tile(D_MODEL),
        out_shape=jax.ShapeDtypeStruct((b, seq, D_MODEL), F32),
        scratch_shapes=[pltpu.VMEM((2, SUBLANES + ts, cw), F32),
                        pltpu.VMEM((n_chunks, SUBLANES, cw), F32)],
        compiler_params=_cparams(("arbitrary", "arbitrary")),
        name="conv_ffn",
    )(h, gf, wup, fw, fb, wdn, p, gp, wpg, wpp, gfin)


def _expand_cmp_weights(w1, w2):
    half = CMP_BLOCK // 2
    w1r = w1.reshape(2, half, HEAD_DIM, CMP_HIDDEN)
    wab = jnp.zeros((2, half, N_KV, HEAD_DIM, N_KV, CMP_HIDDEN), w1.dtype)
    w2e = jnp.zeros((N_KV, CMP_HIDDEN, N_KV, HEAD_DIM), w2.dtype)
    for g in range(N_KV):
        wab = wab.at[:, :, g, :, g, :].set(w1r)
        w2e = w2e.at[g, :, g, :].set(w2)
    wab = wab.reshape(2, half * KV_W, N_KV * CMP_HIDDEN).astype(BF16)
    return wab[0], wab[1], w2e.reshape(N_KV * CMP_HIDDEN, KV_W).astype(BF16)


def _expand_pe(pe):
    half = CMP_BLOCK // 2
    per = jnp.broadcast_to(pe.reshape(2, half, 1, HEAD_DIM), (2, half, N_KV, HEAD_DIM))
    return per.reshape(2, half * KV_W)


def kernel(x, p, positions, norm_mix_g, w_in, pe_k, pe_v, cmp_k_w1, cmp_k_w2, cmp_v_w1, cmp_v_w2, conv_dw_w, conv_dw_b, conv_ln_g, conv_ln_b, w_a, w_b, w_o, norm_ffn_g, w_up, ffn_dw_w, ffn_dw_b, w_down, norm_ple_g, w_ple_gate, w_ple_proj, norm_final_g):
    b, seq, _ = x.shape
    depth = w_in.shape[0]
    n_tok = b * seq
    n_cmp = seq // CMP_STRIDE
    n_sel = seq // SEL_BLOCK
    assert seq % TS_MIX == 0 and seq % TK_SEL == 0 and seq >= WINDOW and n_sel <= HEAD_DIM
    assert FFN_DIM % FFN_CHUNK == 0

    inv = ROPE_THETA ** (-jnp.arange(0, ROT_DIM, 2, dtype=F32) / ROT_DIM)
    d = np.arange(LANES) % HEAD_DIM
    inv_row = jnp.where(d < ROT_DIM, inv[d % (ROT_DIM // 2)], 0.0).reshape(1, LANES)
    sgn_row = jnp.asarray(np.where(d < ROT_DIM // 2, -1.0, 1.0), F32).reshape(1, LANES)
    cmp_start = jnp.arange(n_cmp) * CMP_STRIDE
    sel_start = jnp.arange(n_sel) * SEL_BLOCK
    overlap = jnp.clip(jnp.minimum(cmp_start[:, None] + CMP_BLOCK, sel_start[None, :] + SEL_BLOCK)
                       - jnp.maximum(cmp_start[:, None], sel_start[None, :]), 0).astype(F32) / CMP_STRIDE
    ov_t = jnp.pad(overlap.T, ((0, HEAD_DIM - n_sel), (0, 0)))

    cmp_end = np.minimum(np.arange(n_cmp) * CMP_STRIDE + CMP_BLOCK - 1, seq - 1)
    pos_all = jnp.concatenate([positions.reshape(-1), positions[:, cmp_end].reshape(-1)])
    pad = (-pos_all.shape[0]) % 512
    pos_all = jnp.pad(pos_all, (0, pad)).reshape(-1, 1)
    cos_t, sin_t = _rope_tables(pos_all, inv_row, sgn_row)
    cos_c = cos_t[n_tok:n_tok + b * n_cmp].reshape(b, n_cmp, LANES)
    sin_c = sin_t[n_tok:n_tok + b * n_cmp].reshape(b, n_cmp, LANES)

    h = x
    row = lambda v: v.reshape(1, -1)
    n_chunks = FFN_DIM // FFN_CHUNK
    for i in range(depth):
        w_qkv = jnp.concatenate(
            [w_in[i][:, :QKV_COLS + GATE_COLS],
             jnp.zeros((D_MODEL, LANES - GATE_COLS), w_in.dtype)], axis=1).astype(BF16)
        w_gm = w_in[i][:, QKV_COLS + GATE_COLS:].astype(BF16)

        qpad, kc, vc, *seq_kv, gates = _inproj(
            h.reshape(n_tok, D_MODEL), row(norm_mix_g[i]), w_qkv, cos_t, sin_t, seq)

        wka, wkb, wk2 = _expand_cmp_weights(cmp_k_w1[i], cmp_k_w2[i])
        wva, wvb, wv2 = _expand_cmp_weights(cmp_v_w1[i], cmp_v_w2[i])
        kcmp, vcmp = _compress(kc.reshape(b, n_cmp, CMP_STRIDE * KV_W), vc.reshape(b, n_cmp, CMP_STRIDE * KV_W),
                               _expand_pe(pe_k[i]), _expand_pe(pe_v[i]),
                               wka, wkb, wk2, wva, wvb, wv2, cos_c, sin_c)

        s3 = lambda a: a.reshape(b, seq, a.shape[-1])
        attn = _attention(s3(qpad), kcmp, vcmp, [s3(a) for a in seq_kv], s3(gates), ov_t)

        h = _mix(h, row(norm_mix_g[i]), w_gm, attn,
                 conv_dw_w[i].reshape(CONV_WIDTH, CONV_CH), row(conv_dw_b[i]),
                 row(conv_ln_g[i]), row(conv_ln_b[i]),
                 w_a[i].astype(BF16), w_b[i].astype(BF16), w_o[i].astype(BF16))

        def regroup(a):
            lead = a.shape[:-1]
            a2 = a.reshape(lead + (2, n_chunks, FFN_CHUNK))
            a2 = jnp.moveaxis(a2, -3, -2).reshape(lead + (n_chunks, 2 * FFN_CHUNK))
            return jnp.moveaxis(a2, -2, 0)
        wup = regroup(w_up[i]).astype(BF16)
        fw = regroup(ffn_dw_w[i].reshape(FFN_CONV_WIDTH, 2 * FFN_DIM))
        fb = regroup(ffn_dw_b[i].reshape(1, 2 * FFN_DIM))
        wdn = w_down[i].reshape(n_chunks, FFN_CHUNK, D_MODEL).astype(BF16)
        h = _ffn(h, row(norm_ffn_g[i]), wup, fw, fb, wdn, p[i], row(norm_ple_g[i]),
                 w_ple_gate[i].astype(BF16), w_ple_proj[i].astype(BF16), row(norm_final_g),
                 final_norm=(i == depth - 1))
    return h
```

```python
import functools

import numpy as np
import jax
import jax.numpy as jnp
from jax import lax
from jax.experimental import pallas as pl
from jax.experimental.pallas import tpu as pltpu

D_MODEL = 1024
PLE_DIM = 256
N_HEADS = 8
N_KV = 2
HPG = N_HEADS // N_KV
HEAD_DIM = 64
ROT_DIM = HEAD_DIM // 4
ROPE_THETA = 500000.0
CMP_BLOCK = 32
CMP_STRIDE = 16
CMP_HIDDEN = 128
SEL_BLOCK = 64
SEL_TOPK = 16
WINDOW = 512
CONV_CH = 512
CONV_WIDTH = 31
FFN_DIM = 2816
FFN_CONV_WIDTH = 3
EPS = 1e-6
NEG = -1e30
FORCED_SCORE = 1e6
LOG2_E = 1.4426950408889634

ATT_W = N_HEADS * HEAD_DIM
KV_W = N_KV * HEAD_DIM
QKV_COLS = ATT_W + 6 * KV_W
GATE_COLS = 3 * N_HEADS

LANES = 128
SUBLANES = 8
VMEM_LIMIT_BYTES = 56 * 1024 * 1024

TM_PROJ = 512
TQ = 128
TK_SEL = 512
TS_MIX = 512
TS_FFN = 512
FFN_CHUNK = 256
FFN_DOWN_GROUP = 4
CONV_ROWS = 32

F32 = jnp.float32
BF16 = jnp.bfloat16


def _cparams(sem):
    return pltpu.CompilerParams(dimension_semantics=sem, vmem_limit_bytes=VMEM_LIMIT_BYTES)


def _const_spec(shape):
    nd = len(shape)
    return pl.BlockSpec(shape, lambda *_: (0,) * nd, pipeline_mode=pl.Buffered(1))


def _rms(x, g):
    ms = jnp.mean(x * x, axis=-1, keepdims=True)
    return (x * lax.rsqrt(ms + EPS)) * g


def _rope(v, cos, sin_signed):
    d = lax.broadcasted_iota(jnp.int32, v.shape, 1) & (HEAD_DIM - 1)
    half = ROT_DIM // 2
    sw = jnp.where(d < half, pltpu.roll(v, LANES - half, 1), pltpu.roll(v, half, 1))
    return v * cos + sw * sin_signed


def _perm_rows(ts, w):
    return SUBLANES * ((ts // SUBLANES) * (w // LANES) + 4)


def _perm_access(ts, w):
    n, wc = ts // SUBLANES, w // LANES
    pitch = n * wc + 4
    natural = lambda t0, c: pl.ds((t0 // n) * pitch + (t0 % n) * wc + c, SUBLANES, stride=wc)
    strided = lambda j, c: pl.ds(j * wc + c, SUBLANES, stride=pitch)
    return n, wc, natural, strided


def _reorder(val, pbuf, to_strided):
    ts, w = val.shape
    n, wc, natural, strided = _perm_access(ts, w)
    src, dst = (natural, strided) if to_strided else (strided, natural)
    for g in range(n):
        for c in range(wc):
            pbuf[src(g * SUBLANES, c) if to_strided else src(g, c), :] = (
                val[g * SUBLANES:(g + 1) * SUBLANES, c * LANES:(c + 1) * LANES])
    return jnp.concatenate(
        [jnp.concatenate([pbuf[dst(g, c) if to_strided else dst(g * SUBLANES, c), :] for c in range(wc)], axis=1)
         for g in range(n)], axis=0)


def _shift_in_groups(prev_tail, tail):
    sub = lax.broadcasted_iota(jnp.int32, tail.shape, 0) & (SUBLANES - 1)
    merged = jnp.where(sub == SUBLANES - 1, prev_tail, tail)
    return jnp.concatenate(
        [pltpu.roll(merged[i:i + SUBLANES], 1, 0) for i in range(0, tail.shape[0], SUBLANES)], axis=0)


def _rope_table_kernel(pos_ref, inv_ref, sgn_ref, cos_ref, sin_ref):
    ang = pos_ref[...].astype(F32) * inv_ref[...]
    cos_ref[...] = jnp.cos(ang)
    sin_ref[...] = jnp.sin(ang) * sgn_ref[...]


def _rope_tables(pos_col, inv_row, sgn_row):
    n = pos_col.shape[0]
    tb = 512
    return pl.pallas_call(
        _rope_table_kernel,
        grid=(n // tb,),
        in_specs=[pl.BlockSpec((tb, 1), lambda i: (i, 0)),
                  _const_spec((1, LANES)), _const_spec((1, LANES))],
        out_specs=[pl.BlockSpec((tb, LANES), lambda i: (i, 0))] * 2,
        out_shape=[jax.ShapeDtypeStruct((n, LANES), F32)] * 2,
        compiler_params=_cparams(("parallel",)),
        name="rope_tables",
    )(pos_col, inv_row, sgn_row)


def _inproj_kernel(x_ref, g_ref, w_ref, cos_ref, sin_ref,
                   qpad_ref, kc_ref, vc_ref, ks0_ref, ks1_ref, vs0_ref, vs1_ref, kw_ref, vw0_ref, vw1_ref,
                   gate_ref, *, seq):
    tm = x_ref.shape[0]
    u = _rms(x_ref[...], g_ref[...])
    z = jnp.dot(u.astype(BF16), w_ref[...], preferred_element_type=F32)
    cos = cos_ref[...]
    sin = sin_ref[...]
    lane = lax.broadcasted_iota(jnp.int32, (tm, LANES), 1)
    low = lane < HEAD_DIM

    scale = HEAD_DIM ** -0.5 * LOG2_E
    for c in range(ATT_W // LANES):
        qc = _rope(z[:, c * LANES:(c + 1) * LANES], cos, sin) * scale
        qsw = pltpu.roll(qc, HEAD_DIM, 1)
        for e in range(2):
            hh = 2 * c + e
            grp = hh // HPG
            src = qc if e == grp else qsw
            keep = low if grp == 0 else jnp.logical_not(low)
            qpad_ref[:, hh * LANES:(hh + 1) * LANES] = jnp.where(keep, src, 0.0).astype(BF16)

    o = ATT_W
    kc_ref[...] = z[:, o:o + LANES]
    vc_ref[...] = z[:, o + LANES:o + 2 * LANES]
    ks = _rope(z[:, o + 2 * LANES:o + 3 * LANES], cos, sin)
    kw_ref[...] = _rope(z[:, o + 4 * LANES:o + 5 * LANES], cos, sin).astype(BF16)
    gate_ref[...] = jax.nn.sigmoid(z[:, o + 6 * LANES:o + 7 * LANES])
    for v, (r0, r1) in ((z[:, o + 3 * LANES:o + 4 * LANES], (vs0_ref, vs1_ref)),
                        (z[:, o + 5 * LANES:o + 6 * LANES], (vw0_ref, vw1_ref))):
        r0[...] = jnp.where(low, v, 1.0).astype(BF16)
        r1[...] = jnp.where(low, 1.0, v).astype(BF16)

    s0 = (pl.program_id(0) * tm) % seq
    blk = (s0 + lax.broadcasted_iota(jnp.int32, (tm, LANES), 0)) // SEL_BLOCK
    hot0 = (lane - HEAD_DIM == blk).astype(F32)
    hot1 = (lane == blk).astype(F32)
    ks0_ref[...] = jnp.where(low, ks, hot0).astype(BF16)
    ks1_ref[...] = jnp.where(low, hot1, ks).astype(BF16)


def _inproj(x2, g, w_qkv, cos_t, sin_t, seq):
    n = x2.shape[0]
    tm = TM_PROJ
    ncols = w_qkv.shape[1]
    tok = lambda w: pl.BlockSpec((tm, w), lambda i: (i, 0))
    outs = [(N_HEADS * LANES, BF16), (LANES, F32), (LANES, F32)] + [(LANES, BF16)] * 7 + [(LANES, F32)]
    return pl.pallas_call(
        functools.partial(_inproj_kernel, seq=seq),
        grid=(n // tm,),
        in_specs=[tok(D_MODEL), _const_spec((1, D_MODEL)), _const_spec((D_MODEL, ncols)),
                  tok(LANES), tok(LANES)],
        out_specs=[tok(w) for w, _ in outs],
        out_shape=[jax.ShapeDtypeStruct((n, w), dt) for w, dt in outs],
        compiler_params=_cparams(("parallel",)),
        name="in_proj",
    )(x2, g, w_qkv, cos_t, sin_t)


def _compress_kernel(kx_ref, vx_ref, pek_ref, pev_ref, wka_ref, wkb_ref, wk2_ref,
                     wva_ref, wvb_ref, wv2_ref, cos_ref, sin_ref, kcmp_ref, vcmp_ref):
    def comp(x, pe_ref, wa_ref, wb_ref, w2_ref):
        n = x.shape[0]
        xa = (x + pe_ref[0:1, :]).astype(BF16)
        xb = (x + pe_ref[1:2, :]).astype(BF16)
        ya = jnp.dot(xa, wa_ref[...], preferred_element_type=F32)
        yb = jnp.dot(xb, wb_ref[...], preferred_element_type=F32)
        h = ya + pltpu.roll(yb, n - 1, 0)
        return jnp.dot(jax.nn.gelu(h).astype(BF16), w2_ref[...], preferred_element_type=F32)

    kc = comp(kx_ref[0], pek_ref, wka_ref, wkb_ref, wk2_ref)
    kcmp_ref[0] = _rope(kc, cos_ref[0], sin_ref[0]).astype(BF16)
    vcmp_ref[0] = comp(vx_ref[0], pev_ref, wva_ref, wvb_ref, wv2_ref).astype(BF16)


def _compress(kx, vx, pek, pev, wka, wkb, wk2, wva, wvb, wv2, cos_c, sin_c):
    b, n, w = kx.shape
    per_b = lambda s: pl.BlockSpec((1,) + s, lambda i: (i, 0, 0))
    return pl.pallas_call(
        _compress_kernel,
        grid=(b,),
        in_specs=[per_b((n, w)), per_b((n, w)), _const_spec(pek.shape), _const_spec(pev.shape),
                  _const_spec(wka.shape), _const_spec(wkb.shape), _const_spec(wk2.shape),
                  _const_spec(wva.shape), _const_spec(wvb.shape), _const_spec(wv2.shape),
                  per_b((n, LANES)), per_b((n, LANES))],
        out_specs=[per_b((n, LANES))] * 2,
        out_shape=[jax.ShapeDtypeStruct((b, n, LANES), BF16)] * 2,
        compiler_params=_cparams(("parallel",)),
        name="kv_compress",
    )(kx, vx, pek, pev, wka, wkb, wk2, wva, wvb, wv2, cos_c, sin_c)


def _nt_dot(a, b):
    return lax.dot_general(a, b, (((1,), (1,)), ((), ())), preferred_element_type=F32)


def _softmax_step(s, v, m, acc):
    m_new = jnp.maximum(m, jnp.max(s, axis=-1, keepdims=True))
    p = jnp.exp2(s - m_new)
    acc_new = jnp.exp2(m - m_new) * acc + jnp.dot(p.astype(BF16), v, preferred_element_type=F32)
    return m_new, acc_new


def _attn_kernel(q_ref, kcmp_ref, vcmp_ref, ks0_ref, ks1_ref, vs0_ref, vs1_ref, kw_ref, vw0_ref, vw1_ref,
                 gate_ref, ovt_ref, out_ref, sbuf, *, seq):
    tq = q_ref.shape[1]
    n_cmp = kcmp_ref.shape[1]
    n_sel = seq // SEL_BLOCK
    top_n = min(SEL_TOPK, n_sel)
    rows = HPG * tq
    groups = range(N_KV)
    q0 = pl.multiple_of(pl.program_id(1) * tq, tq)
    t_col = q0 + lax.broadcasted_iota(jnp.int32, (tq, 1), 0)
    t_row = q0 + lax.broadcasted_iota(jnp.int32, (1, tq), 1)
    low = lax.broadcasted_iota(jnp.int32, (tq, LANES), 1) < HEAD_DIM
    low4 = lax.broadcasted_iota(jnp.int32, (rows, LANES), 1) < HEAD_DIM
    in_grp4 = (low4, jnp.logical_not(low4))
    ks_refs, vs_refs, vw_refs = (ks0_ref, ks1_ref), (vs0_ref, vs1_ref), (vw0_ref, vw1_ref)

    def per_head(mask, s, fill):
        s3 = s.reshape(HPG, tq, s.shape[-1])
        return jnp.where(mask[None], s3, fill).reshape(s.shape)

    cmp_end = lax.broadcasted_iota(jnp.int32, (1, n_cmp), 1) * CMP_STRIDE + (CMP_BLOCK - 1)
    cmp_valid = cmp_end <= t_col
    kcmp = kcmp_ref[0]
    vcmp = vcmp_ref[0]
    gates = gate_ref[0]
    qz = [jnp.concatenate(
        [q_ref[0, :, (g * HPG + h) * LANES:(g * HPG + h + 1) * LANES] for h in range(HPG)], axis=0)
        for g in groups]

    def normalize(acc, g):
        den = HEAD_DIM if g == 0 else 0
        return acc / acc[:, den:den + 1]

    init = (jnp.full((rows, 1), NEG, F32), jnp.zeros((rows, LANES), F32))

    sub = lax.broadcasted_iota(jnp.int32, (SUBLANES, tq), 0)
    cur = t_row // SEL_BLOCK
    n_blk_rows = n_sel // SUBLANES
    o_cmp, bias_t = [], []
    for g in groups:
        s = per_head(cmp_valid, _nt_dot(qz[g], kcmp), NEG)
        m = jnp.max(s, axis=-1, keepdims=True)
        e = jnp.exp2(s - m)
        p = per_head(cmp_valid, e / jnp.sum(e, axis=-1, keepdims=True), 0.0)
        o_cmp.append(jnp.dot(p.astype(BF16), vcmp, preferred_element_type=F32))
        p3 = p.reshape(HPG, tq, n_cmp)
        p_sum = p3[0]
        for h in range(1, HPG):
            p_sum = p_sum + p3[h]
        imp_t = lax.dot_general(ovt_ref[...], p_sum, (((1,), (1,)), ((), ())),
                                preferred_element_type=F32, precision=lax.Precision.HIGHEST)

        score = []
        for v in range(n_blk_rows):
            jj = sub + v * SUBLANES
            forced = (jj == 0) | (jj == cur) | (jj == cur - 1)
            blk_valid = jj * SEL_BLOCK <= t_row
            score.append(jnp.where(blk_valid,
                                   jnp.where(forced, FORCED_SCORE, imp_t[v * SUBLANES:(v + 1) * SUBLANES]), -1.0))
        rank = [jnp.zeros((SUBLANES, tq), jnp.int32)] * n_blk_rows
        for i in range(n_sel):
            vi, ri = divmod(i, SUBLANES)
            si = score[vi][ri:ri + 1, :]
            for v in range(n_blk_rows):
                ge = jnp.where(si >= score[v], 1, 0)
                gt = jnp.where(si > score[v], 1, 0)
                rank[v] = rank[v] + (ge if v > vi else gt if v < vi else jnp.where(sub > ri, ge, gt))
        blocks = [jnp.where(r >= top_n, NEG, 0.0) for r in rank]
        blocks += [jnp.zeros((SUBLANES, tq), F32)] * (HEAD_DIM // SUBLANES - n_blk_rows)
        bias_t.append(jnp.concatenate(blocks, axis=0))

    bias = jnp.concatenate([bias_t[1], bias_t[0]], axis=0).T.astype(BF16)
    bias4 = jnp.concatenate([bias] * HPG, axis=0)
    qb = [jnp.where(in_grp4[g], qz[g], bias4) for g in groups]

    def put_scores(start, slot):
        for g in groups:
            sbuf[slot, g] = _nt_dot(qb[g], ks_refs[g][0, pl.ds(start, TK_SEL), :])

    def consume(start, s, state):
        out = []
        for g in groups:
            out.extend(_softmax_step(s[g], vs_refs[g][0, pl.ds(start, TK_SEL), :], state[2 * g], state[2 * g + 1]))
        return tuple(out)

    def sel_step(start, slot, state):
        put_scores(start + TK_SEL, 1 - slot)
        return consume(start, [sbuf[slot, g] for g in groups], state)

    def sel_pair(kp, state):
        start = pl.multiple_of(kp * (2 * TK_SEL), 2 * TK_SEL)
        return sel_step(start + TK_SEL, 1, sel_step(start, 0, state))

    n_full = q0 // TK_SEL
    n_pair = n_full // 2
    put_scores(0, 0)
    state = lax.fori_loop(0, n_pair, sel_pair, init * N_KV)
    state = lax.fori_loop(
        n_pair * 2, n_full, lambda kt, st: sel_step(pl.multiple_of(kt * TK_SEL, TK_SEL), 0, st), state)
    last = pl.multiple_of(n_full * TK_SEL, TK_SEL)
    kpos = last + lax.broadcasted_iota(jnp.int32, (1, TK_SEL), 1)
    s_last = [per_head(kpos <= t_col, sbuf[n_full % 2, g], NEG) for g in groups]
    state = consume(last, s_last, state)
    o_sel = [normalize(state[2 * g + 1], g) for g in groups]

    a0 = pl.multiple_of(jnp.maximum(q0 - WINDOW, 0), tq)
    dmask = q0 + lax.broadcasted_iota(jnp.int32, (1, tq), 1) <= t_col
    kpos = a0 + lax.broadcasted_iota(jnp.int32, (1, WINDOW), 1)
    wmask = (kpos < q0) & (t_col - kpos < WINDOW)
    o_win = []
    for g in groups:
        s = per_head(dmask, _nt_dot(qz[g], kw_ref[0, pl.ds(q0, tq), :]), NEG)
        c = _softmax_step(s, vw_refs[g][0, pl.ds(q0, tq), :], *init)
        s = per_head(wmask, _nt_dot(qz[g], kw_ref[0, pl.ds(a0, WINDOW), :]), NEG)
        o_win.append(normalize(_softmax_step(s, vw_refs[g][0, pl.ds(a0, WINDOW), :], *c)[1], g))

    for g in groups:
        heads = []
        for h in range(HPG):
            col = g * HPG + h
            r = slice(h * tq, (h + 1) * tq)
            heads.append(gates[:, col:col + 1] * o_cmp[g][r]
                         + gates[:, N_HEADS + col:N_HEADS + col + 1] * o_sel[g][r]
                         + gates[:, 2 * N_HEADS + col:2 * N_HEADS + col + 1] * o_win[g][r])
        for c in range(HPG // 2):
            a, b = heads[2 * c], heads[2 * c + 1]
            if g == 0:
                pair = jnp.where(low, a, pltpu.roll(b, HEAD_DIM, 1))
            else:
                pair = jnp.where(low, pltpu.roll(a, HEAD_DIM, 1), b)
            oc = g * (HPG // 2) + c
            out_ref[0, :, oc * LANES:(oc + 1) * LANES] = pair.astype(out_ref.dtype)


def _attention(qpad, kcmp, vcmp, seq_kv, gates, ov_t):
    b, seq, _ = qpad.shape
    n_cmp = kcmp.shape[1]
    q_tile = lambda w: pl.BlockSpec((1, TQ, w), lambda i, j: (i, j, 0))
    per_b = lambda r: pl.BlockSpec((1, r, LANES), lambda i, j: (i, 0, 0))
    return pl.pallas_call(
        functools.partial(_attn_kernel, seq=seq),
        grid=(b, seq // TQ),
        in_specs=[q_tile(N_HEADS * LANES), per_b(n_cmp), per_b(n_cmp)] + [per_b(seq)] * len(seq_kv)
                 + [q_tile(LANES), _const_spec(ov_t.shape)],
        out_specs=q_tile(ATT_W),
        out_shape=jax.ShapeDtypeStruct((b, seq, ATT_W), BF16),
        scratch_shapes=[pltpu.VMEM((2, N_KV, HPG * TQ, TK_SEL), F32)],
        compiler_params=_cparams(("parallel", "parallel")),
        name="nsa_attention",
    )(qpad, kcmp, vcmp, *seq_kv, gates, ov_t)


def _mix_kernel(x_ref, g_ref, wgm_ref, attn_ref, dww_ref, dwb_ref, lng_ref, lnb_ref,
                wa_ref, wb_ref, wo_ref, h_ref, cbuf, carry, ybuf, pbuf):
    ts = x_ref.shape[1]
    head = (CONV_WIDTH - 1) * SUBLANES

    @pl.when(pl.program_id(1) == 0)
    def _():
        carry[...] = jnp.zeros_like(carry)

    x = x_ref[0]
    u = _rms(x, g_ref[...]).astype(BF16)
    o = 2 * CONV_CH
    zg = jnp.dot(u, wgm_ref[:, :o], preferred_element_type=F32)
    glu = _reorder(zg[:, :CONV_CH] * jax.nn.sigmoid(zg[:, CONV_CH:]), pbuf, True)

    attn = attn_ref[0]
    blk = 2 * LANES
    side = [functools.partial(lambda n0: jnp.dot(u, wgm_ref[:, o + n0:o + n0 + blk], preferred_element_type=F32), n0)
            for n0 in range(0, 2 * D_MODEL, blk)]
    side += [functools.partial(lambda n0: jnp.dot(attn, wa_ref[:, n0:n0 + blk], preferred_element_type=F32), n0)
             for n0 in range(0, D_MODEL, blk)]
    side_out = []

    tail = glu[ts - head:, :]
    cbuf[0:head, :] = _shift_in_groups(carry[...], tail)
    cbuf[head:head + ts, :] = glu
    carry[...] = tail
    sub_groups = CONV_ROWS // SUBLANES
    for i, r0 in enumerate(range(0, ts, CONV_ROWS)):
        acc = jnp.broadcast_to(dwb_ref[...][None], (sub_groups, SUBLANES, CONV_CH))
        for k in range(CONV_WIDTH):
            tap = cbuf[r0 + k * SUBLANES:r0 + k * SUBLANES + CONV_ROWS, :]
            acc = acc + dww_ref[k * SUBLANES:(k + 1) * SUBLANES, :][None] * tap.reshape(sub_groups, SUBLANES, CONV_CH)
        ybuf[r0:r0 + CONV_ROWS, :] = acc.reshape(CONV_ROWS, CONV_CH)
        if i < len(side):
            side_out.append(side[i]())
    side_out += [f() for f in side[len(side_out):]]
    zm = jnp.concatenate(side_out[:2 * D_MODEL // blk], axis=1)
    y_a = jnp.concatenate(side_out[2 * D_MODEL // blk:], axis=1)

    y = _reorder(ybuf[...], pbuf, False)
    mu = jnp.mean(y, axis=-1, keepdims=True)
    var = jnp.mean(jnp.square(y - mu), axis=-1, keepdims=True)
    yn = (y - mu) * lax.rsqrt(var + EPS) * lng_ref[...] + lnb_ref[...]
    c = yn * jax.nn.sigmoid(yn)
    y_b = jnp.dot(c.astype(BF16), wb_ref[...], preferred_element_type=F32)
    mix = jax.nn.sigmoid(zm[:, :D_MODEL]) * y_a + jax.nn.sigmoid(zm[:, D_MODEL:]) * y_b
    h_ref[0] = x + jnp.dot(mix.astype(BF16), wo_ref[...], preferred_element_type=F32)


def _mix(x, g, wgm, attn, dww, dwb, lng, lnb, wa, wb, wo):
    b, seq, _ = x.shape
    ts = TS_MIX
    tile = lambda w: pl.BlockSpec((1, ts, w), lambda i, j: (i, j, 0))
    consts = [g, wgm]
    consts2 = [dww, dwb, lng, lnb, wa, wb, wo]
    return pl.pallas_call(
        _mix_kernel,
        grid=(b, seq // ts),
        in_specs=[tile(D_MODEL)] + [_const_spec(a.shape) for a in consts]
                 + [tile(ATT_W)] + [_const_spec(a.shape) for a in consts2],
        out_specs=tile(D_MODEL),
        out_shape=jax.ShapeDtypeStruct((b, seq, D_MODEL), F32),
        scratch_shapes=[pltpu.VMEM(((CONV_WIDTH - 1) * SUBLANES + ts, CONV_CH), F32),
                        pltpu.VMEM(((CONV_WIDTH - 1) * SUBLANES, CONV_CH), F32),
                        pltpu.VMEM((ts, CONV_CH), F32),
                        pltpu.VMEM((_perm_rows(ts, CONV_CH), LANES), F32)],
        compiler_params=_cparams(("arbitrary", "arbitrary")),
        name="conv_merge",
    )(x, g, wgm, attn, dww, dwb, lng, lnb, wa, wb, wo)


def _ffn_kernel(h_ref, gf_ref, wup_ref, fw_ref, fb_ref, wdn_ref, p_ref, gp_ref, wpg_ref, wpp_ref,
                gfin_ref, out_ref, ubuf, carry, actbuf, hbuf, pbuf, *, final_norm):
    ts = h_ref.shape[1]
    fc = FFN_CHUNK
    n_chunks = FFN_DIM // fc
    head = (FFN_CONV_WIDTH - 1) * SUBLANES

    @pl.when(pl.program_id(1) == 0)
    def _():
        carry[...] = jnp.zeros_like(carry)

    h = _reorder(h_ref[0], hbuf, True)
    u = _rms(h, gf_ref[...]).astype(BF16)

    def conv_half(c, col0, slot):
        cols = slice(col0 + c * fc, col0 + (c + 1) * fc)
        up = jnp.dot(u, wup_ref[:, cols], preferred_element_type=F32)
        tail = up[ts - head:, :]
        buf = ubuf.at[slot]
        buf[0:head, :] = _shift_in_groups(carry[:, cols], tail)
        buf[head:head + ts, :] = up
        carry[:, cols] = tail
        conv = fb_ref[:, cols]
        for k in range(FFN_CONV_WIDTH):
            conv = conv + fw_ref[k:k + 1, cols] * buf[k * SUBLANES:k * SUBLANES + ts, :]
        return conv

    h2 = h
    for c0 in range(0, n_chunks, FFN_DOWN_GROUP):
        c1 = min(c0 + FFN_DOWN_GROUP, n_chunks)
        for c in range(c0, c1):
            a = conv_half(c, 0, (2 * c) % 4)
            gt = conv_half(c, FFN_DIM, (2 * c + 1) % 4)
            actbuf[:, c * fc:(c + 1) * fc] = (gt * jax.nn.sigmoid(gt) * a).astype(BF16)
        h2 = h2 + jnp.dot(actbuf[:, c0 * fc:c1 * fc], wdn_ref[c0 * fc:c1 * fc, :], preferred_element_type=F32)

    u3 = _rms(h2, gp_ref[...]).astype(BF16)
    gate = jax.nn.sigmoid(jnp.dot(u3, wpg_ref[...], preferred_element_type=F32))
    proj = jnp.dot(_reorder(p_ref[0], pbuf, True).astype(BF16), wpp_ref[...], preferred_element_type=F32)
    h3 = h2 + gate * proj
    out_ref[0] = _reorder(_rms(h3, gfin_ref[...]) if final_norm else h3, hbuf, False)


def _ffn(h, gf, wup, fw, fb, wdn, p, gp, wpg, wpp, gfin, final_norm):
    b, seq, _ = h.shape
    ts = TS_FFN
    tile = lambda w: pl.BlockSpec((1, ts, w), lambda i, j: (i, j, 0))
    single = lambda a: _const_spec(a.shape)
    return pl.pallas_call(
        functools.partial(_ffn_kernel, final_norm=final_norm),
        grid=(b, seq // ts),
        in_specs=[tile(D_MODEL), single(gf), single(wup), single(fw), single(fb), single(wdn),
                  tile(PLE_DIM), single(gp), single(wpg), single(wpp), single(gfin)],
        out_specs=tile(D_MODEL),
        out_shape=jax.ShapeDtypeStruct((b, seq, D_MODEL), F32),
        scratch_shapes=[pltpu.VMEM((4, (FFN_CONV_WIDTH - 1) * SUBLANES + ts, FFN_CHUNK), F32),
                        pltpu.VMEM(((FFN_CONV_WIDTH - 1) * SUBLANES, 2 * FFN_DIM), F32),
                        pltpu.VMEM((ts, FFN_DIM), BF16),
                        pltpu.VMEM((_perm_rows(ts, D_MODEL), LANES), F32),
                        pltpu.VMEM((_perm_rows(ts, PLE_DIM), LANES), F32)],
        compiler_params=_cparams(("arbitrary", "arbitrary")),
        name="conv_ffn",
    )(h, gf, wup, fw, fb, wdn, p, gp, wpg, wpp, gfin)


def _expand_cmp_weights(w1, w2):
    half = CMP_BLOCK // 2
    w1r = w1.reshape(2, half, HEAD_DIM, CMP_HIDDEN).astype(BF16)
    w2b = w2.astype(BF16)
    z1, z2 = jnp.zeros_like(w1r), jnp.zeros_like(w2b)
    wab = jnp.stack([jnp.concatenate([w1r, z1], axis=-1), jnp.concatenate([z1, w1r], axis=-1)], axis=2)
    wab = wab.reshape(2, half * KV_W, N_KV * CMP_HIDDEN)
    w2e = jnp.concatenate([jnp.concatenate([w2b, z2], axis=-1), jnp.concatenate([z2, w2b], axis=-1)], axis=0)
    return wab[0], wab[1], w2e


def _expand_pe(pe):
    half = CMP_BLOCK // 2
    per = jnp.broadcast_to(pe.reshape(2, half, 1, HEAD_DIM), (2, half, N_KV, HEAD_DIM))
    return per.reshape(2, half * KV_W)


def kernel(x, p, positions, norm_mix_g, w_in, pe_k, pe_v, cmp_k_w1, cmp_k_w2, cmp_v_w1, cmp_v_w2, conv_dw_w, conv_dw_b, conv_ln_g, conv_ln_b, w_a, w_b, w_o, norm_ffn_g, w_up, ffn_dw_w, ffn_dw_b, w_down, norm_ple_g, w_ple_gate, w_ple_proj, norm_final_g):
    b, seq, _ = x.shape
    depth = w_in.shape[0]
    n_tok = b * seq
    n_cmp = seq // CMP_STRIDE
    n_sel = seq // SEL_BLOCK
    assert seq % TS_MIX == 0 and seq % TK_SEL == 0 and seq >= WINDOW and n_sel <= HEAD_DIM
    assert FFN_DIM % FFN_CHUNK == 0 and TS_MIX == TS_FFN and N_KV == 2

    inv = ROPE_THETA ** (-jnp.arange(0, ROT_DIM, 2, dtype=F32) / ROT_DIM)
    d = np.arange(LANES) % HEAD_DIM
    inv_row = jnp.where(d < ROT_DIM, inv[d % (ROT_DIM // 2)], 0.0).reshape(1, LANES)
    sgn_row = jnp.asarray(np.where(d < ROT_DIM // 2, -1.0, 1.0), F32).reshape(1, LANES)
    cmp_start = jnp.arange(n_cmp) * CMP_STRIDE
    sel_start = jnp.arange(n_sel) * SEL_BLOCK
    overlap = jnp.clip(jnp.minimum(cmp_start[:, None] + CMP_BLOCK, sel_start[None, :] + SEL_BLOCK)
                       - jnp.maximum(cmp_start[:, None], sel_start[None, :]), 0).astype(F32) / CMP_STRIDE
    ov_t = jnp.pad(overlap.T, ((0, HEAD_DIM - n_sel), (0, 0)))

    cmp_end = np.minimum(np.arange(n_cmp) * CMP_STRIDE + CMP_BLOCK - 1, seq - 1)
    pos_all = jnp.concatenate([positions.reshape(-1), positions[:, cmp_end].reshape(-1)])
    pad = (-pos_all.shape[0]) % 512
    pos_all = jnp.pad(pos_all, (0, pad)).reshape(-1, 1)
    cos_t, sin_t = _rope_tables(pos_all, inv_row, sgn_row)
    cos_c = cos_t[n_tok:n_tok + b * n_cmp].reshape(b, n_cmp, LANES)
    sin_c = sin_t[n_tok:n_tok + b * n_cmp].reshape(b, n_cmp, LANES)

    h = x
    row = lambda v: v.reshape(1, -1)
    for i in range(depth):
        w_qkv = jnp.concatenate(
            [w_in[i][:, :QKV_COLS + GATE_COLS],
             jnp.zeros((D_MODEL, LANES - GATE_COLS), w_in.dtype)], axis=1).astype(BF16)
        w_gm = w_in[i][:, QKV_COLS + GATE_COLS:].astype(BF16)

        qpad, kc, vc, *seq_kv, gates = _inproj(
            h.reshape(n_tok, D_MODEL), row(norm_mix_g[i]), w_qkv, cos_t, sin_t, seq)

        wka, wkb, wk2 = _expand_cmp_weights(cmp_k_w1[i], cmp_k_w2[i])
        wva, wvb, wv2 = _expand_cmp_weights(cmp_v_w1[i], cmp_v_w2[i])
        kcmp, vcmp = _compress(kc.reshape(b, n_cmp, CMP_STRIDE * KV_W), vc.reshape(b, n_cmp, CMP_STRIDE * KV_W),
                               _expand_pe(pe_k[i]), _expand_pe(pe_v[i]),
                               wka, wkb, wk2, wva, wvb, wv2, cos_c, sin_c)

        s3 = lambda a: a.reshape(b, seq, a.shape[-1])
        attn = _attention(s3(qpad), kcmp, vcmp, [s3(a) for a in seq_kv], s3(gates), ov_t)

        h = _mix(h, row(norm_mix_g[i]), w_gm, attn,
                 jnp.repeat(conv_dw_w[i].reshape(CONV_WIDTH, CONV_CH), SUBLANES, axis=0),
                 jnp.broadcast_to(row(conv_dw_b[i]), (SUBLANES, CONV_CH)),
                 row(conv_ln_g[i]), row(conv_ln_b[i]),
                 w_a[i].astype(BF16), w_b[i].astype(BF16), w_o[i].astype(BF16))

        h = _ffn(h, row(norm_ffn_g[i]), w_up[i].astype(BF16),
                 ffn_dw_w[i].reshape(FFN_CONV_WIDTH, 2 * FFN_DIM), row(ffn_dw_b[i]),
                 w_down[i].astype(BF16), p[i], row(norm_ple_g[i]),
                 w_ple_gate[i].astype(BF16), w_ple_proj[i].astype(BF16), row(norm_final_g),
                 final_norm=(i == depth - 1))
    return h
```

```python
import functools

import numpy as np
import jax
import jax.numpy as jnp
from jax import lax
from jax.experimental import pallas as pl
from jax.experimental.pallas import tpu as pltpu

D_MODEL = 1024
PLE_DIM = 256
N_HEADS = 8
N_KV = 2
HPG = N_HEADS // N_KV
HEAD_DIM = 64
ROT_DIM = HEAD_DIM // 4
ROPE_THETA = 500000.0
CMP_BLOCK = 32
CMP_STRIDE = 16
CMP_HIDDEN = 128
SEL_BLOCK = 64
SEL_TOPK = 16
WINDOW = 512
CONV_CH = 512
CONV_WIDTH = 31
FFN_DIM = 2816
FFN_CONV_WIDTH = 3
EPS = 1e-6
NEG = -1e30
FORCED_SCORE = 1e6
LOG2_E = 1.4426950408889634

ATT_W = N_HEADS * HEAD_DIM
KV_W = N_KV * HEAD_DIM
QKV_COLS = ATT_W + 6 * KV_W
GATE_COLS = 3 * N_HEADS

LANES = 128
SUBLANES = 8
VMEM_LIMIT_BYTES = 56 * 1024 * 1024

TM_PROJ = 512
TQ = 128
TK_SEL = 512
TS_MIX = 512
TS_FFN = 512
FFN_CHUNK = 256
FFN_DOWN_GROUP = 4
CONV_ROWS = 32

F32 = jnp.float32
BF16 = jnp.bfloat16


def _cparams(sem):
    return pltpu.CompilerParams(dimension_semantics=sem, vmem_limit_bytes=VMEM_LIMIT_BYTES)


def _const_spec(shape):
    nd = len(shape)
    return pl.BlockSpec(shape, lambda *_: (0,) * nd, pipeline_mode=pl.Buffered(1))


def _rms(x, g):
    ms = jnp.mean(x * x, axis=-1, keepdims=True)
    return (x * lax.rsqrt(ms + EPS)) * g


def _rope(v, cos, sin_signed):
    d = lax.broadcasted_iota(jnp.int32, v.shape, 1) & (HEAD_DIM - 1)
    half = ROT_DIM // 2
    sw = jnp.where(d < half, pltpu.roll(v, LANES - half, 1), pltpu.roll(v, half, 1))
    return v * cos + sw * sin_signed


def _perm_rows(ts, w):
    return SUBLANES * ((ts // SUBLANES) * (w // LANES) + 4)


def _perm_access(ts, w):
    n, wc = ts // SUBLANES, w // LANES
    pitch = n * wc + 4
    natural = lambda t0, c: pl.ds((t0 // n) * pitch + (t0 % n) * wc + c, SUBLANES, stride=wc)
    strided = lambda j, c: pl.ds(j * wc + c, SUBLANES, stride=pitch)
    return n, wc, natural, strided


def _reorder(val, pbuf, to_strided):
    ts, w = val.shape
    n, wc, natural, strided = _perm_access(ts, w)
    src, dst = (natural, strided) if to_strided else (strided, natural)
    for g in range(n):
        for c in range(wc):
            pbuf[src(g * SUBLANES, c) if to_strided else src(g, c), :] = (
                val[g * SUBLANES:(g + 1) * SUBLANES, c * LANES:(c + 1) * LANES])
    return jnp.concatenate(
        [jnp.concatenate([pbuf[dst(g, c) if to_strided else dst(g * SUBLANES, c), :] for c in range(wc)], axis=1)
         for g in range(n)], axis=0)


def _dependent_zero(x, reps):
    bits = pltpu.bitcast(x, jnp.uint32)
    zero = pltpu.bitcast(lax.shift_right_logical(lax.shift_right_logical(bits, jnp.uint32(16)), jnp.uint32(16)), F32)
    return jnp.concatenate([zero] * reps, axis=1)


def _shift_in_groups(prev_tail, tail):
    sub = lax.broadcasted_iota(jnp.int32, tail.shape, 0) & (SUBLANES - 1)
    merged = jnp.where(sub == SUBLANES - 1, prev_tail, tail)
    return jnp.concatenate(
        [pltpu.roll(merged[i:i + SUBLANES], 1, 0) for i in range(0, tail.shape[0], SUBLANES)], axis=0)


def _rope_table_kernel(pos_ref, inv_ref, sgn_ref, cos_ref, sin_ref):
    ang = pos_ref[...].astype(F32) * inv_ref[...]
    cos_ref[...] = jnp.cos(ang)
    sin_ref[...] = jnp.sin(ang) * sgn_ref[...]


def _rope_tables(pos_col, inv_row, sgn_row):
    n = pos_col.shape[0]
    tb = 512
    return pl.pallas_call(
        _rope_table_kernel,
        grid=(n // tb,),
        in_specs=[pl.BlockSpec((tb, 1), lambda i: (i, 0)),
                  _const_spec((1, LANES)), _const_spec((1, LANES))],
        out_specs=[pl.BlockSpec((tb, LANES), lambda i: (i, 0))] * 2,
        out_shape=[jax.ShapeDtypeStruct((n, LANES), F32)] * 2,
        compiler_params=_cparams(("parallel",)),
        name="rope_tables",
    )(pos_col, inv_row, sgn_row)


def _inproj_kernel(x_ref, g_ref, w_ref, cos_ref, sin_ref,
                   qpad_ref, kc_ref, vc_ref, ks0_ref, ks1_ref, vs0_ref, vs1_ref, kw_ref, vw0_ref, vw1_ref,
                   gate_ref, *, seq):
    tm = x_ref.shape[0]
    u = _rms(x_ref[...], g_ref[...])
    z = jnp.dot(u.astype(BF16), w_ref[...], preferred_element_type=F32)
    cos = cos_ref[...]
    sin = sin_ref[...]
    lane = lax.broadcasted_iota(jnp.int32, (tm, LANES), 1)
    low = lane < HEAD_DIM

    scale = HEAD_DIM ** -0.5 * LOG2_E
    for c in range(ATT_W // LANES):
        qc = _rope(z[:, c * LANES:(c + 1) * LANES], cos, sin) * scale
        qsw = pltpu.roll(qc, HEAD_DIM, 1)
        for e in range(2):
            hh = 2 * c + e
            grp = hh // HPG
            src = qc if e == grp else qsw
            keep = low if grp == 0 else jnp.logical_not(low)
            qpad_ref[:, hh * LANES:(hh + 1) * LANES] = jnp.where(keep, src, 0.0).astype(BF16)

    o = ATT_W
    kc_ref[...] = z[:, o:o + LANES]
    vc_ref[...] = z[:, o + LANES:o + 2 * LANES]
    ks = _rope(z[:, o + 2 * LANES:o + 3 * LANES], cos, sin)
    kw_ref[...] = _rope(z[:, o + 4 * LANES:o + 5 * LANES], cos, sin).astype(BF16)
    sig = jax.nn.sigmoid(z[:, o + 6 * LANES:o + 7 * LANES])
    gate_ref[...] = jnp.where(low, sig, pltpu.roll(sig, HEAD_DIM, 1))
    for v, (r0, r1) in ((z[:, o + 3 * LANES:o + 4 * LANES], (vs0_ref, vs1_ref)),
                        (z[:, o + 5 * LANES:o + 6 * LANES], (vw0_ref, vw1_ref))):
        r0[...] = jnp.where(low, v, 1.0).astype(BF16)
        r1[...] = jnp.where(low, 1.0, v).astype(BF16)

    s0 = (pl.program_id(0) * tm) % seq
    blk = (s0 + lax.broadcasted_iota(jnp.int32, (tm, LANES), 0)) // SEL_BLOCK
    hot0 = (lane - HEAD_DIM == blk).astype(F32)
    hot1 = (lane == blk).astype(F32)
    ks0_ref[...] = jnp.where(low, ks, hot0).astype(BF16)
    ks1_ref[...] = jnp.where(low, hot1, ks).astype(BF16)


def _inproj(x2, g, w_qkv, cos_t, sin_t, seq):
    n = x2.shape[0]
    tm = TM_PROJ
    ncols = w_qkv.shape[1]
    tok = lambda w: pl.BlockSpec((tm, w), lambda i: (i, 0))
    outs = [(N_HEADS * LANES, BF16), (LANES, F32), (LANES, F32)] + [(LANES, BF16)] * 7 + [(LANES, F32)]
    return pl.pallas_call(
        functools.partial(_inproj_kernel, seq=seq),
        grid=(n // tm,),
        in_specs=[tok(D_MODEL), _const_spec((1, D_MODEL)), _const_spec((D_MODEL, ncols)),
                  tok(LANES), tok(LANES)],
        out_specs=[tok(w) for w, _ in outs],
        out_shape=[jax.ShapeDtypeStruct((n, w), dt) for w, dt in outs],
        compiler_params=_cparams(("parallel",)),
        name="in_proj",
    )(x2, g, w_qkv, cos_t, sin_t)


def _compress_kernel(kx_ref, vx_ref, pek_ref, pev_ref, wka_ref, wkb_ref, wk2_ref,
                     wva_ref, wvb_ref, wv2_ref, cos_ref, sin_ref, kcmp_ref, vcmp_ref):
    def comp(x, pe_ref, wa_ref, wb_ref, w2_ref):
        n = x.shape[0]
        xa = (x + pe_ref[0:1, :]).astype(BF16)
        xb = (x + pe_ref[1:2, :]).astype(BF16)
        ya = jnp.dot(xa, wa_ref[...], preferred_element_type=F32)
        yb = jnp.dot(xb, wb_ref[...], preferred_element_type=F32)
        h = ya + pltpu.roll(yb, n - 1, 0)
        return jnp.dot(jax.nn.gelu(h).astype(BF16), w2_ref[...], preferred_element_type=F32)

    def stride_rows(ref):
        n = ref.shape[1] // CMP_STRIDE
        return jnp.concatenate([ref[0, pl.ds(r, n, stride=CMP_STRIDE), :] for r in range(CMP_STRIDE)], axis=1)

    kc = comp(stride_rows(kx_ref), pek_ref, wka_ref, wkb_ref, wk2_ref)
    kcmp_ref[0] = _rope(kc, cos_ref[...], sin_ref[...]).astype(BF16)
    vcmp_ref[0] = comp(stride_rows(vx_ref), pev_ref, wva_ref, wvb_ref, wv2_ref).astype(BF16)


def _compress(kx, vx, pek, pev, wka, wkb, wk2, wva, wvb, wv2, cos_t, sin_t, row0):
    b, seq, w = kx.shape
    n = seq // CMP_STRIDE
    per_b = lambda s: pl.BlockSpec((1,) + s, lambda i: (i, 0, 0))
    table = pl.BlockSpec((n, LANES), lambda i: (row0 // n + i, 0))
    return pl.pallas_call(
        _compress_kernel,
        grid=(b,),
        in_specs=[per_b((seq, w)), per_b((seq, w)), _const_spec(pek.shape), _const_spec(pev.shape),
                  _const_spec(wka.shape), _const_spec(wkb.shape), _const_spec(wk2.shape),
                  _const_spec(wva.shape), _const_spec(wvb.shape), _const_spec(wv2.shape),
                  table, table],
        out_specs=[per_b((n, LANES))] * 2,
        out_shape=[jax.ShapeDtypeStruct((b, n, LANES), BF16)] * 2,
        compiler_params=_cparams(("parallel",)),
        name="kv_compress",
    )(kx, vx, pek, pev, wka, wkb, wk2, wva, wvb, wv2, cos_t, sin_t)


def _nt_dot(a, b):
    return lax.dot_general(a, b, (((1,), (1,)), ((), ())), preferred_element_type=F32)


def _softmax_step(s, v, m, acc):
    m_new = jnp.maximum(m, jnp.max(s, axis=-1, keepdims=True))
    p = jnp.exp2(s - m_new)
    acc_new = jnp.exp2(m - m_new) * acc + jnp.dot(p.astype(BF16), v, preferred_element_type=F32)
    return m_new, acc_new


def _attn_kernel(q_ref, kcmp_ref, vcmp_ref, ks0_ref, ks1_ref, vs0_ref, vs1_ref, kw_ref, vw0_ref, vw1_ref,
                 gate_ref, ovt_ref, out_ref, sbuf, *, seq):
    tq = q_ref.shape[1]
    n_cmp = kcmp_ref.shape[1]
    n_sel = seq // SEL_BLOCK
    top_n = min(SEL_TOPK, n_sel)
    rows = HPG * tq
    groups = range(N_KV)
    q0 = pl.multiple_of(pl.program_id(1) * tq, tq)
    t_col = q0 + lax.broadcasted_iota(jnp.int32, (tq, 1), 0)
    t_row = q0 + lax.broadcasted_iota(jnp.int32, (1, tq), 1)
    low = lax.broadcasted_iota(jnp.int32, (tq, LANES), 1) < HEAD_DIM
    low4 = lax.broadcasted_iota(jnp.int32, (rows, LANES), 1) < HEAD_DIM
    in_grp4 = (low4, jnp.logical_not(low4))
    ks_refs, vs_refs, vw_refs = (ks0_ref, ks1_ref), (vs0_ref, vs1_ref), (vw0_ref, vw1_ref)

    def per_head(mask, s, fill):
        s3 = s.reshape(HPG, tq, s.shape[-1])
        return jnp.where(mask[None], s3, fill).reshape(s.shape)

    cmp_end = lax.broadcasted_iota(jnp.int32, (1, n_cmp), 1) * CMP_STRIDE + (CMP_BLOCK - 1)
    cmp_valid = cmp_end <= t_col
    kcmp = kcmp_ref[0]
    vcmp = vcmp_ref[0]
    gates = gate_ref[0]
    qz = [jnp.concatenate(
        [q_ref[0, :, (g * HPG + h) * LANES:(g * HPG + h + 1) * LANES] for h in range(HPG)], axis=0)
        for g in groups]

    init = (jnp.full((rows, 1), NEG, F32), jnp.zeros((rows, LANES), F32))

    sub = lax.broadcasted_iota(jnp.int32, (SUBLANES, tq), 0)
    cur = t_row // SEL_BLOCK
    n_blk_rows = n_sel // SUBLANES
    o_cmp, bias_t = [], []
    for g in groups:
        s = per_head(cmp_valid, _nt_dot(qz[g], kcmp), NEG)
        m = jnp.max(s, axis=-1, keepdims=True)
        e = jnp.exp2(s - m)
        p = per_head(cmp_valid, e / jnp.sum(e, axis=-1, keepdims=True), 0.0)
        o_cmp.append(jnp.dot(p.astype(BF16), vcmp, preferred_element_type=F32))
        p3 = p.reshape(HPG, tq, n_cmp)
        p_sum = p3[0]
        for h in range(1, HPG):
            p_sum = p_sum + p3[h]
        imp_t = lax.dot_general(ovt_ref[...], p_sum, (((1,), (1,)), ((), ())),
                                preferred_element_type=F32, precision=lax.Precision.HIGHEST)

        score = []
        for v in range(n_blk_rows):
            jj = sub + v * SUBLANES
            forced = (jj == 0) | (jj == cur) | (jj == cur - 1)
            blk_valid = jj * SEL_BLOCK <= t_row
            score.append(jnp.where(blk_valid,
                                   jnp.where(forced, FORCED_SCORE, imp_t[v * SUBLANES:(v + 1) * SUBLANES]), -1.0))
        rank = [jnp.zeros((SUBLANES, tq), jnp.int32)] * n_blk_rows
        for i in range(n_sel):
            vi, ri = divmod(i, SUBLANES)
            si = score[vi][ri:ri + 1, :]
            for v in range(n_blk_rows):
                ge = jnp.where(si >= score[v], 1, 0)
                gt = jnp.where(si > score[v], 1, 0)
                rank[v] = rank[v] + (ge if v > vi else gt if v < vi else jnp.where(sub > ri, ge, gt))
        blocks = [jnp.where(r >= top_n, NEG, 0.0) for r in rank]
        blocks += [jnp.zeros((SUBLANES, tq), F32)] * (HEAD_DIM // SUBLANES - n_blk_rows)
        bias_t.append(jnp.concatenate(blocks, axis=0))

    bias = jnp.concatenate([bias_t[1], bias_t[0]], axis=0).T.astype(BF16)
    bias4 = jnp.concatenate([bias] * HPG, axis=0)
    qb = [jnp.where(in_grp4[g], qz[g], bias4) for g in groups]

    a0 = pl.multiple_of(jnp.maximum(q0 - WINDOW, 0), tq)
    kpos = a0 + lax.broadcasted_iota(jnp.int32, (1, WINDOW + tq), 1)
    wmask = (kpos <= t_col) & (t_col - kpos < WINDOW)
    o_win = []
    for g in groups:
        s = per_head(wmask, _nt_dot(qz[g], kw_ref[0, pl.ds(a0, WINDOW + tq), :]), NEG)
        o_win.append(_softmax_step(s, vw_refs[g][0, pl.ds(a0, WINDOW + tq), :], *init)[1])

    def put_scores(start, slot):
        for g in groups:
            sbuf[slot, g] = _nt_dot(qb[g], ks_refs[g][0, pl.ds(start, TK_SEL), :])

    def consume(start, s, state):
        out = []
        for g in groups:
            out.extend(_softmax_step(s[g], vs_refs[g][0, pl.ds(start, TK_SEL), :], state[2 * g], state[2 * g + 1]))
        return tuple(out)

    def sel_step(start, slot, state):
        put_scores(start + TK_SEL, 1 - slot)
        return consume(start, [sbuf[slot, g] for g in groups], state)

    def sel_pair(kp, state):
        start = pl.multiple_of(kp * (2 * TK_SEL), 2 * TK_SEL)
        return sel_step(start + TK_SEL, 1, sel_step(start, 0, state))

    n_full = q0 // TK_SEL
    n_pair = n_full // 2
    put_scores(0, 0)
    state = lax.fori_loop(0, n_pair, sel_pair, init * N_KV)
    state = lax.fori_loop(
        n_pair * 2, n_full, lambda kt, st: sel_step(pl.multiple_of(kt * TK_SEL, TK_SEL), 0, st), state)
    last = pl.multiple_of(n_full * TK_SEL, TK_SEL)
    kpos = last + lax.broadcasted_iota(jnp.int32, (1, TK_SEL), 1)
    s_last = [per_head(kpos <= t_col, sbuf[n_full % 2, g], NEG) for g in groups]
    state = consume(last, s_last, state)
    o_sel = [state[2 * g + 1] for g in groups]

    for g in groups:
        heads = []
        for h in range(HPG):
            col = (HEAD_DIM if g == 0 else 0) + g * HPG + h
            r = slice(h * tq, (h + 1) * tq)
            w_sel = gates / o_sel[g][r]
            w_win = gates / o_win[g][r]
            heads.append(gates[:, col:col + 1] * o_cmp[g][r]
                         + w_sel[:, N_HEADS + col:N_HEADS + col + 1] * o_sel[g][r]
                         + w_win[:, 2 * N_HEADS + col:2 * N_HEADS + col + 1] * o_win[g][r])
        for c in range(HPG // 2):
            a, b = heads[2 * c], heads[2 * c + 1]
            if g == 0:
                pair = jnp.where(low, a, pltpu.roll(b, HEAD_DIM, 1))
            else:
                pair = jnp.where(low, pltpu.roll(a, HEAD_DIM, 1), b)
            oc = g * (HPG // 2) + c
            out_ref[0, :, oc * LANES:(oc + 1) * LANES] = pair.astype(out_ref.dtype)


def _attention(qpad, kcmp, vcmp, seq_kv, gates, ov_t):
    b, seq, _ = qpad.shape
    n_cmp = kcmp.shape[1]
    q_tile = lambda w: pl.BlockSpec((1, TQ, w), lambda i, j: (i, j, 0))
    per_b = lambda r: pl.BlockSpec((1, r, LANES), lambda i, j: (i, 0, 0))
    return pl.pallas_call(
        functools.partial(_attn_kernel, seq=seq),
        grid=(b, seq // TQ),
        in_specs=[q_tile(N_HEADS * LANES), per_b(n_cmp), per_b(n_cmp)] + [per_b(seq)] * len(seq_kv)
                 + [q_tile(LANES), _const_spec(ov_t.shape)],
        out_specs=q_tile(ATT_W),
        out_shape=jax.ShapeDtypeStruct((b, seq, ATT_W), BF16),
        scratch_shapes=[pltpu.VMEM((2, N_KV, HPG * TQ, TK_SEL), F32)],
        compiler_params=_cparams(("parallel", "parallel")),
        name="nsa_attention",
    )(qpad, kcmp, vcmp, *seq_kv, gates, ov_t)


def _mix_kernel(x_ref, g_ref, wgm_ref, attn_ref, dww_ref, dwb_ref, lng_ref, lnb_ref,
                wa_ref, wb_ref, wo_ref, h_ref, cbuf, carry, ybuf, pbuf):
    ts = x_ref.shape[1]
    head = (CONV_WIDTH - 1) * SUBLANES

    @pl.when(pl.program_id(1) == 0)
    def _():
        carry[...] = jnp.zeros_like(carry)

    x = x_ref[0]
    u = _rms(x, g_ref[...]).astype(BF16)
    o = 2 * CONV_CH
    zg = jnp.dot(u, wgm_ref[:, :o], preferred_element_type=F32)
    glu = _reorder(zg[:, :CONV_CH] * jax.nn.sigmoid(zg[:, CONV_CH:]), pbuf, True)

    attn = attn_ref[0]
    blk = 2 * LANES
    side = [functools.partial(lambda n0: jnp.dot(u, wgm_ref[:, o + n0:o + n0 + blk], preferred_element_type=F32), n0)
            for n0 in range(0, 2 * D_MODEL, blk)]
    side += [functools.partial(lambda n0: jnp.dot(attn, wa_ref[:, n0:n0 + blk], preferred_element_type=F32), n0)
             for n0 in range(0, D_MODEL, blk)]
    side_out = []

    tail = glu[ts - head:, :]
    cbuf[0:head, :] = _shift_in_groups(carry[...], tail)
    cbuf[head:head + ts, :] = glu
    carry[...] = tail
    sub_groups = CONV_ROWS // SUBLANES
    pace = jnp.zeros((SUBLANES, CONV_CH), F32)
    for i, r0 in enumerate(range(0, ts, CONV_ROWS)):
        acc = [dwb_ref[...] + pace] + [dwb_ref[...]] * (sub_groups - 1)
        wts = [dww_ref[k * SUBLANES:(k + 1) * SUBLANES, :] for k in range(CONV_WIDTH)]
        for m in range(CONV_WIDTH - 1 + sub_groups):
            grp = cbuf[r0 + m * SUBLANES:r0 + (m + 1) * SUBLANES, :]
            for a in range(sub_groups):
                if 0 <= m - a < CONV_WIDTH:
                    acc[a] = acc[a] + wts[m - a] * grp
        ybuf[r0:r0 + CONV_ROWS, :] = jnp.concatenate(acc, axis=0)
        if i < len(side):
            side_out.append(side[i]())
            pace = _dependent_zero(side_out[-1][ts - SUBLANES:, :], CONV_CH // blk)
    side_out += [f() for f in side[len(side_out):]]
    zm = jnp.concatenate(side_out[:2 * D_MODEL // blk], axis=1)
    y_a = jnp.concatenate(side_out[2 * D_MODEL // blk:], axis=1)

    y = _reorder(ybuf[...], pbuf, False)
    mu = jnp.mean(y, axis=-1, keepdims=True)
    var = jnp.mean(jnp.square(y - mu), axis=-1, keepdims=True)
    yn = (y - mu) * lax.rsqrt(var + EPS) * lng_ref[...] + lnb_ref[...]
    c = yn * jax.nn.sigmoid(yn)
    y_b = jnp.dot(c.astype(BF16), wb_ref[...], preferred_element_type=F32)
    mix = jax.nn.sigmoid(zm[:, :D_MODEL]) * y_a + jax.nn.sigmoid(zm[:, D_MODEL:]) * y_b
    h_ref[0] = x + jnp.dot(mix.astype(BF16), wo_ref[...], preferred_element_type=F32)


def _mix(x, g, wgm, attn, dww, dwb, lng, lnb, wa, wb, wo):
    b, seq, _ = x.shape
    ts = TS_MIX
    tile = lambda w: pl.BlockSpec((1, ts, w), lambda i, j: (i, j, 0))
    consts = [g, wgm]
    consts2 = [dww, dwb, lng, lnb, wa, wb, wo]
    return pl.pallas_call(
        _mix_kernel,
        grid=(b, seq // ts),
        in_specs=[tile(D_MODEL)] + [_const_spec(a.shape) for a in consts]
                 + [tile(ATT_W)] + [_const_spec(a.shape) for a in consts2],
        out_specs=tile(D_MODEL),
        out_shape=jax.ShapeDtypeStruct((b, seq, D_MODEL), F32),
        scratch_shapes=[pltpu.VMEM(((CONV_WIDTH - 1) * SUBLANES + ts, CONV_CH), F32),
                        pltpu.VMEM(((CONV_WIDTH - 1) * SUBLANES, CONV_CH), F32),
                        pltpu.VMEM((ts, CONV_CH), F32),
                        pltpu.VMEM((_perm_rows(ts, CONV_CH), LANES), F32)],
        compiler_params=_cparams(("arbitrary", "arbitrary")),
        name="conv_merge",
    )(x, g, wgm, attn, dww, dwb, lng, lnb, wa, wb, wo)


def _ffn_kernel(h_ref, gf_ref, wup_ref, fw_ref, fb_ref, wdn_ref, p_ref, gp_ref, wpg_ref, wpp_ref,
                gfin_ref, out_ref, ubuf, carry, actbuf, hbuf, pbuf, *, final_norm):
    ts = h_ref.shape[1]
    fc = FFN_CHUNK
    n_chunks = FFN_DIM // fc
    head = (FFN_CONV_WIDTH - 1) * SUBLANES

    @pl.when(pl.program_id(1) == 0)
    def _():
        carry[...] = jnp.zeros_like(carry)

    h = _reorder(h_ref[0], hbuf, True)
    u = _rms(h, gf_ref[...]).astype(BF16)

    def conv_half(c, col0, slot):
        cols = slice(col0 + c * fc, col0 + (c + 1) * fc)
        up = jnp.dot(u, wup_ref[:, cols], preferred_element_type=F32)
        tail = up[ts - head:, :]
        buf = ubuf.at[slot]
        buf[0:head, :] = _shift_in_groups(carry[:, cols], tail)
        buf[head:head + ts, :] = up
        carry[:, cols] = tail
        conv = fb_ref[:, cols]
        for k in range(FFN_CONV_WIDTH):
            conv = conv + fw_ref[k:k + 1, cols] * buf[k * SUBLANES:k * SUBLANES + ts, :]
        return conv

    h2 = h
    for c0 in range(0, n_chunks, FFN_DOWN_GROUP):
        c1 = min(c0 + FFN_DOWN_GROUP, n_chunks)
        for c in range(c0, c1):
            a = conv_half(c, 0, (2 * c) % 4)
            gt = conv_half(c, FFN_DIM, (2 * c + 1) % 4)
            actbuf[:, c * fc:(c + 1) * fc] = (gt * jax.nn.sigmoid(gt) * a).astype(BF16)
        h2 = h2 + jnp.dot(actbuf[:, c0 * fc:c1 * fc], wdn_ref[c0 * fc:c1 * fc, :], preferred_element_type=F32)

    u3 = _rms(h2, gp_ref[...]).astype(BF16)
    gate = jax.nn.sigmoid(jnp.dot(u3, wpg_ref[...], preferred_element_type=F32))
    proj = jnp.dot(_reorder(p_ref[0], pbuf, True).astype(BF16), wpp_ref[...], preferred_element_type=F32)
    h3 = h2 + gate * proj
    out_ref[0] = _reorder(_rms(h3, gfin_ref[...]) if final_norm else h3, hbuf, False)


def _ffn(h, gf, wup, fw, fb, wdn, p, gp, wpg, wpp, gfin, final_norm):
    b, seq, _ = h.shape
    ts = TS_FFN
    tile = lambda w: pl.BlockSpec((1, ts, w), lambda i, j: (i, j, 0))
    single = lambda a: _const_spec(a.shape)
    return pl.pallas_call(
        functools.partial(_ffn_kernel, final_norm=final_norm),
        grid=(b, seq // ts),
        in_specs=[tile(D_MODEL), single(gf), single(wup), single(fw), single(fb), single(wdn),
                  tile(PLE_DIM), single(gp), single(wpg), single(wpp), single(gfin)],
        out_specs=tile(D_MODEL),
        out_shape=jax.ShapeDtypeStruct((b, seq, D_MODEL), F32),
        scratch_shapes=[pltpu.VMEM((4, (FFN_CONV_WIDTH - 1) * SUBLANES + ts, FFN_CHUNK), F32),
                        pltpu.VMEM(((FFN_CONV_WIDTH - 1) * SUBLANES, 2 * FFN_DIM), F32),
                        pltpu.VMEM((ts, FFN_DIM), BF16),
                        pltpu.VMEM((_perm_rows(ts, D_MODEL), LANES), F32),
                        pltpu.VMEM((_perm_rows(ts, PLE_DIM), LANES), F32)],
        compiler_params=_cparams(("arbitrary", "arbitrary")),
        name="conv_ffn",
    )(h, gf, wup, fw, fb, wdn, p, gp, wpg, wpp, gfin)


def _expand_cmp_weights(w1, w2):
    half = CMP_BLOCK // 2
    w1r = w1.reshape(2, half, HEAD_DIM, CMP_HIDDEN).astype(BF16)
    w2b = w2.astype(BF16)
    z1, z2 = jnp.zeros_like(w1r), jnp.zeros_like(w2b)
    wab = jnp.stack([jnp.concatenate([w1r, z1], axis=-1), jnp.concatenate([z1, w1r], axis=-1)], axis=2)
    wab = wab.reshape(2, half * KV_W, N_KV * CMP_HIDDEN)
    w2e = jnp.concatenate([jnp.concatenate([w2b, z2], axis=-1), jnp.concatenate([z2, w2b], axis=-1)], axis=0)
    return wab[0], wab[1], w2e


def _expand_pe(pe):
    half = CMP_BLOCK // 2
    per = jnp.broadcast_to(pe.reshape(2, half, 1, HEAD_DIM), (2, half, N_KV, HEAD_DIM))
    return per.reshape(2, half * KV_W)


def kernel(x, p, positions, norm_mix_g, w_in, pe_k, pe_v, cmp_k_w1, cmp_k_w2, cmp_v_w1, cmp_v_w2, conv_dw_w, conv_dw_b, conv_ln_g, conv_ln_b, w_a, w_b, w_o, norm_ffn_g, w_up, ffn_dw_w, ffn_dw_b, w_down, norm_ple_g, w_ple_gate, w_ple_proj, norm_final_g):
    b, seq, _ = x.shape
    depth = w_in.shape[0]
    n_tok = b * seq
    n_cmp = seq // CMP_STRIDE
    n_sel = seq // SEL_BLOCK
    assert seq % TS_MIX == 0 and seq % TK_SEL == 0 and seq >= WINDOW and n_sel <= HEAD_DIM
    assert FFN_DIM % FFN_CHUNK == 0 and seq % TS_FFN == 0 and N_KV == 2

    inv = ROPE_THETA ** (-jnp.arange(0, ROT_DIM, 2, dtype=F32) / ROT_DIM)
    d = np.arange(LANES) % HEAD_DIM
    inv_row = jnp.where(d < ROT_DIM, inv[d % (ROT_DIM // 2)], 0.0).reshape(1, LANES)
    sgn_row = jnp.asarray(np.where(d < ROT_DIM // 2, -1.0, 1.0), F32).reshape(1, LANES)
    cmp_start = jnp.arange(n_cmp) * CMP_STRIDE
    sel_start = jnp.arange(n_sel) * SEL_BLOCK
    overlap = jnp.clip(jnp.minimum(cmp_start[:, None] + CMP_BLOCK, sel_start[None, :] + SEL_BLOCK)
                       - jnp.maximum(cmp_start[:, None], sel_start[None, :]), 0).astype(F32) / CMP_STRIDE
    ov_t = jnp.pad(overlap.T, ((0, HEAD_DIM - n_sel), (0, 0)))

    cmp_end = np.minimum(np.arange(n_cmp) * CMP_STRIDE + CMP_BLOCK - 1, seq - 1)
    pos_all = jnp.concatenate([positions.reshape(-1), positions[:, cmp_end].reshape(-1)])
    pad = (-pos_all.shape[0]) % 512
    pos_all = jnp.pad(pos_all, (0, pad)).reshape(-1, 1)
    cos_t, sin_t = _rope_tables(pos_all, inv_row, sgn_row)
    assert n_tok % n_cmp == 0

    h = x
    row = lambda v: v.reshape(1, -1)
    for i in range(depth):
        w_qkv = jnp.concatenate(
            [w_in[i][:, :QKV_COLS + GATE_COLS],
             jnp.zeros((D_MODEL, LANES - GATE_COLS), w_in.dtype)], axis=1).astype(BF16)
        w_gm = w_in[i][:, QKV_COLS + GATE_COLS:].astype(BF16)

        qpad, kc, vc, *seq_kv, gates = _inproj(
            h.reshape(n_tok, D_MODEL), row(norm_mix_g[i]), w_qkv, cos_t, sin_t, seq)

        wka, wkb, wk2 = _expand_cmp_weights(cmp_k_w1[i], cmp_k_w2[i])
        wva, wvb, wv2 = _expand_cmp_weights(cmp_v_w1[i], cmp_v_w2[i])
        kcmp, vcmp = _compress(kc.reshape(b, seq, KV_W), vc.reshape(b, seq, KV_W),
                               _expand_pe(pe_k[i]), _expand_pe(pe_v[i]),
                               wka, wkb, wk2, wva, wvb, wv2, cos_t, sin_t, n_tok)

        s3 = lambda a: a.reshape(b, seq, a.shape[-1])
        attn = _attention(s3(qpad), kcmp, vcmp, [s3(a) for a in seq_kv], s3(gates), ov_t)

        h = _mix(h, row(norm_mix_g[i]), w_gm, attn,
                 jnp.repeat(conv_dw_w[i].reshape(CONV_WIDTH, CONV_CH), SUBLANES, axis=0),
                 jnp.broadcast_to(row(conv_dw_b[i]), (SUBLANES, CONV_CH)),
                 row(conv_ln_g[i]), row(conv_ln_b[i]),
                 w_a[i].astype(BF16), w_b[i].astype(BF16), w_o[i].astype(BF16))

        h = _ffn(h, row(norm_ffn_g[i]), w_up[i].astype(BF16),
                 ffn_dw_w[i].reshape(FFN_CONV_WIDTH, 2 * FFN_DIM), row(ffn_dw_b[i]),
                 w_down[i].astype(BF16), p[i], row(norm_ple_g[i]),
                 w_ple_gate[i].astype(BF16), w_ple_proj[i].astype(BF16), row(norm_final_g),
                 final_norm=(i == depth - 1))
    return h
```

```python
import functools

import numpy as np
import jax
import jax.numpy as jnp
from jax import lax
from jax.experimental import pallas as pl
from jax.experimental.pallas import tpu as pltpu

D_MODEL = 1024
PLE_DIM = 256
N_HEADS = 8
N_KV = 2
HPG = N_HEADS // N_KV
HEAD_DIM = 64
ROT_DIM = HEAD_DIM // 4
ROPE_THETA = 500000.0
CMP_BLOCK = 32
CMP_STRIDE = 16
CMP_HIDDEN = 128
SEL_BLOCK = 64
SEL_TOPK = 16
WINDOW = 512
CONV_CH = 512
CONV_WIDTH = 31
FFN_DIM = 2816
FFN_CONV_WIDTH = 3
EPS = 1e-6
NEG = -1e30
FORCED_SCORE = 1e6
LOG2_E = 1.4426950408889634

ATT_W = N_HEADS * HEAD_DIM
KV_W = N_KV * HEAD_DIM
QKV_COLS = ATT_W + 6 * KV_W
GATE_COLS = 3 * N_HEADS

LANES = 128
SUBLANES = 8
VMEM_LIMIT_BYTES = 56 * 1024 * 1024

TM_PROJ = 512
TQ = 128
TK_SEL = 512
TS_MIX = 512
TS_FFN = 512
FFN_CHUNK = 256
FFN_DOWN_GROUP = 4
CONV_ROWS = 32

F32 = jnp.float32
BF16 = jnp.bfloat16


def _cparams(sem):
    return pltpu.CompilerParams(dimension_semantics=sem, vmem_limit_bytes=VMEM_LIMIT_BYTES)


def _const_spec(shape):
    nd = len(shape)
    return pl.BlockSpec(shape, lambda *_: (0,) * nd, pipeline_mode=pl.Buffered(1))


def _rms(x, g):
    ms = jnp.mean(x * x, axis=-1, keepdims=True)
    return (x * lax.rsqrt(ms + EPS)) * g


def _rope(v, cos, sin_signed):
    d = lax.broadcasted_iota(jnp.int32, v.shape, 1) & (HEAD_DIM - 1)
    half = ROT_DIM // 2
    sw = jnp.where(d < half, pltpu.roll(v, LANES - half, 1), pltpu.roll(v, half, 1))
    return v * cos + sw * sin_signed


def _perm_rows(ts, w):
    return SUBLANES * ((ts // SUBLANES) * (w // LANES) + 4)


def _perm_access(ts, w):
    n, wc = ts // SUBLANES, w // LANES
    pitch = n * wc + 4
    natural = lambda t0, c: pl.ds((t0 // n) * pitch + (t0 % n) * wc + c, SUBLANES, stride=wc)
    strided = lambda j, c: pl.ds(j * wc + c, SUBLANES, stride=pitch)
    return n, wc, natural, strided


def _reorder(val, pbuf, to_strided):
    ts, w = val.shape
    n, wc, natural, strided = _perm_access(ts, w)
    src, dst = (natural, strided) if to_strided else (strided, natural)
    for g in range(n):
        for c in range(wc):
            pbuf[src(g * SUBLANES, c) if to_strided else src(g, c), :] = (
                val[g * SUBLANES:(g + 1) * SUBLANES, c * LANES:(c + 1) * LANES])
    return jnp.concatenate(
        [jnp.concatenate([pbuf[dst(g, c) if to_strided else dst(g * SUBLANES, c), :] for c in range(wc)], axis=1)
         for g in range(n)], axis=0)


def _dependent_zero(x, reps):
    bits = pltpu.bitcast(x, jnp.uint32)
    zero = pltpu.bitcast(lax.shift_right_logical(lax.shift_right_logical(bits, jnp.uint32(16)), jnp.uint32(16)), F32)
    return jnp.concatenate([zero] * reps, axis=1)


def _shift_in_groups(prev_tail, tail):
    sub = lax.broadcasted_iota(jnp.int32, tail.shape, 0) & (SUBLANES - 1)
    merged = jnp.where(sub == SUBLANES - 1, prev_tail, tail)
    return jnp.concatenate(
        [pltpu.roll(merged[i:i + SUBLANES], 1, 0) for i in range(0, tail.shape[0], SUBLANES)], axis=0)


ROPE_TB = 2048
ROPE_ROWS = 4 * (ROT_DIM // 2)


def _rope_table_kernel(pos_ref, inv_ref, tab_ref):
    half = ROT_DIM // 2
    ang = pos_ref[...].astype(F32) * inv_ref[...]
    sin = jnp.sin(ang)
    tab_ref[0:half, :] = jnp.cos(ang)
    tab_ref[half:2 * half, :] = -sin
    tab_ref[2 * half:3 * half, :] = sin
    sub = lax.broadcasted_iota(jnp.int32, (half, ang.shape[1]), 0)
    tab_ref[3 * half:, :] = jnp.where(sub == 0, 1.0, 0.0)


def _rope_tables(pos_row, inv_col):
    n = pos_row.shape[1]
    return pl.pallas_call(
        _rope_table_kernel,
        grid=(n // ROPE_TB,),
        in_specs=[pl.BlockSpec((1, ROPE_TB), lambda i: (0, i)), _const_spec(inv_col.shape)],
        out_specs=pl.BlockSpec((ROPE_ROWS, ROPE_TB), lambda i: (0, i)),
        out_shape=jax.ShapeDtypeStruct((ROPE_ROWS, n), F32),
        compiler_params=_cparams(("parallel",)),
        name="rope_tables",
    )(pos_row, inv_col)


def _inproj_kernel(x_ref, g_ref, w_ref, cos_ref, sin_ref,
                   qpad_ref, kc_ref, vc_ref, ks0_ref, ks1_ref, vs0_ref, vs1_ref, kw_ref, vw0_ref, vw1_ref,
                   gate_ref, *, seq):
    tm = x_ref.shape[0]
    u = _rms(x_ref[...], g_ref[...])
    z = jnp.dot(u.astype(BF16), w_ref[...], preferred_element_type=F32)
    cos = cos_ref[...]
    sin = sin_ref[...]
    lane = lax.broadcasted_iota(jnp.int32, (tm, LANES), 1)
    low = lane < HEAD_DIM

    scale = HEAD_DIM ** -0.5 * LOG2_E
    for c in range(ATT_W // LANES):
        qc = _rope(z[:, c * LANES:(c + 1) * LANES], cos, sin) * scale
        qsw = pltpu.roll(qc, HEAD_DIM, 1)
        for e in range(2):
            hh = 2 * c + e
            grp = hh // HPG
            src = qc if e == grp else qsw
            keep = low if grp == 0 else jnp.logical_not(low)
            qpad_ref[:, hh * LANES:(hh + 1) * LANES] = jnp.where(keep, src, 0.0).astype(BF16)

    o = ATT_W
    kc_ref[...] = z[:, o:o + LANES]
    vc_ref[...] = z[:, o + LANES:o + 2 * LANES]
    ks = _rope(z[:, o + 2 * LANES:o + 3 * LANES], cos, sin)
    kw_ref[...] = _rope(z[:, o + 4 * LANES:o + 5 * LANES], cos, sin).astype(BF16)
    sig = jax.nn.sigmoid(z[:, o + 6 * LANES:o + 7 * LANES])
    gate_ref[...] = jnp.where(low, sig, pltpu.roll(sig, HEAD_DIM, 1))
    for v, (r0, r1) in ((z[:, o + 3 * LANES:o + 4 * LANES], (vs0_ref, vs1_ref)),
                        (z[:, o + 5 * LANES:o + 6 * LANES], (vw0_ref, vw1_ref))):
        r0[...] = jnp.where(low, v, 1.0).astype(BF16)
        r1[...] = jnp.where(low, 1.0, v).astype(BF16)

    s0 = (pl.program_id(0) * tm) % seq
    blk = (s0 + lax.broadcasted_iota(jnp.int32, (tm, LANES), 0)) // SEL_BLOCK
    hot0 = (lane - HEAD_DIM == blk).astype(F32)
    hot1 = (lane == blk).astype(F32)
    ks0_ref[...] = jnp.where(low, ks, hot0).astype(BF16)
    ks1_ref[...] = jnp.where(low, hot1, ks).astype(BF16)


def _inproj(x2, g, w_qkv, cos_t, sin_t, seq):
    n = x2.shape[0]
    tm = TM_PROJ
    ncols = w_qkv.shape[1]
    tok = lambda w: pl.BlockSpec((tm, w), lambda i: (i, 0))
    outs = [(N_HEADS * LANES, BF16), (LANES, F32), (LANES, F32)] + [(LANES, BF16)] * 7 + [(LANES, F32)]
    return pl.pallas_call(
        functools.partial(_inproj_kernel, seq=seq),
        grid=(n // tm,),
        in_specs=[tok(D_MODEL), _const_spec((1, D_MODEL)), _const_spec((D_MODEL, ncols)),
                  tok(LANES), tok(LANES)],
        out_specs=[tok(w) for w, _ in outs],
        out_shape=[jax.ShapeDtypeStruct((n, w), dt) for w, dt in outs],
        compiler_params=_cparams(("parallel",)),
        name="in_proj",
    )(x2, g, w_qkv, cos_t, sin_t)


def _compress_kernel(kx_ref, vx_ref, pek_ref, pev_ref, wka_ref, wkb_ref, wk2_ref,
                     wva_ref, wvb_ref, wv2_ref, cos_ref, sin_ref, kcmp_ref, vcmp_ref):
    def comp(x, pe_ref, wa_ref, wb_ref, w2_ref):
        n = x.shape[0]
        xa = (x + pe_ref[0:1, :]).astype(BF16)
        xb = (x + pe_ref[1:2, :]).astype(BF16)
        ya = jnp.dot(xa, wa_ref[...], preferred_element_type=F32)
        yb = jnp.dot(xb, wb_ref[...], preferred_element_type=F32)
        h = ya + pltpu.roll(yb, n - 1, 0)
        return jnp.dot(jax.nn.gelu(h).astype(BF16), w2_ref[...], preferred_element_type=F32)

    def stride_rows(ref):
        n = ref.shape[1] // CMP_STRIDE
        return jnp.concatenate([ref[0, pl.ds(r, n, stride=CMP_STRIDE), :] for r in range(CMP_STRIDE)], axis=1)

    kc = comp(stride_rows(kx_ref), pek_ref, wka_ref, wkb_ref, wk2_ref)
    kcmp_ref[0] = _rope(kc, cos_ref[...], sin_ref[...]).astype(BF16)
    vcmp_ref[0] = comp(stride_rows(vx_ref), pev_ref, wva_ref, wvb_ref, wv2_ref).astype(BF16)


def _compress(kx, vx, pek, pev, wka, wkb, wk2, wva, wvb, wv2, cos_t, sin_t, row0):
    b, seq, w = kx.shape
    n = seq // CMP_STRIDE
    per_b = lambda s: pl.BlockSpec((1,) + s, lambda i: (i, 0, 0))
    table = pl.BlockSpec((n, LANES), lambda i: (row0 // n + i, 0))
    return pl.pallas_call(
        _compress_kernel,
        grid=(b,),
        in_specs=[per_b((seq, w)), per_b((seq, w)), _const_spec(pek.shape), _const_spec(pev.shape),
                  _const_spec(wka.shape), _const_spec(wkb.shape), _const_spec(wk2.shape),
                  _const_spec(wva.shape), _const_spec(wvb.shape), _const_spec(wv2.shape),
                  table, table],
        out_specs=[per_b((n, LANES))] * 2,
        out_shape=[jax.ShapeDtypeStruct((b, n, LANES), BF16)] * 2,
        compiler_params=_cparams(("parallel",)),
        name="kv_compress",
    )(kx, vx, pek, pev, wka, wkb, wk2, wva, wvb, wv2, cos_t, sin_t)


def _nt_dot(a, b):
    return lax.dot_general(a, b, (((1,), (1,)), ((), ())), preferred_element_type=F32)


def _softmax_step(s, v, m, acc):
    m_new = jnp.maximum(m, jnp.max(s, axis=-1, keepdims=True))
    p = jnp.exp2(s - m_new)
    acc_new = jnp.exp2(m - m_new) * acc + jnp.dot(p.astype(BF16), v, preferred_element_type=F32)
    return m_new, acc_new


def _attn_kernel(q_ref, kcmp_ref, vcmp_ref, ks0_ref, ks1_ref, vs0_ref, vs1_ref, kw_ref, vw0_ref, vw1_ref,
                 gate_ref, ovt_ref, out_ref, sbuf, *, seq):
    tq = q_ref.shape[1]
    n_cmp = kcmp_ref.shape[1]
    n_sel = seq // SEL_BLOCK
    top_n = min(SEL_TOPK, n_sel)
    rows = HPG * tq
    groups = range(N_KV)
    q0 = pl.multiple_of(pl.program_id(1) * tq, tq)
    t_col = q0 + lax.broadcasted_iota(jnp.int32, (tq, 1), 0)
    t_row = q0 + lax.broadcasted_iota(jnp.int32, (1, tq), 1)
    low = lax.broadcasted_iota(jnp.int32, (tq, LANES), 1) < HEAD_DIM
    low4 = lax.broadcasted_iota(jnp.int32, (rows, LANES), 1) < HEAD_DIM
    in_grp4 = (low4, jnp.logical_not(low4))
    ks_refs, vs_refs, vw_refs = (ks0_ref, ks1_ref), (vs0_ref, vs1_ref), (vw0_ref, vw1_ref)

    def per_head(mask, s, fill):
        s3 = s.reshape(HPG, tq, s.shape[-1])
        return jnp.where(mask[None], s3, fill).reshape(s.shape)

    cmp_end = lax.broadcasted_iota(jnp.int32, (1, n_cmp), 1) * CMP_STRIDE + (CMP_BLOCK - 1)
    cmp_valid = cmp_end <= t_col
    kcmp = kcmp_ref[0]
    vcmp = vcmp_ref[0]
    gates = gate_ref[0]
    qz = [jnp.concatenate(
        [q_ref[0, :, (g * HPG + h) * LANES:(g * HPG + h + 1) * LANES] for h in range(HPG)], axis=0)
        for g in groups]

    init = (jnp.full((rows, 1), NEG, F32), jnp.zeros((rows, LANES), F32))

    sub = lax.broadcasted_iota(jnp.int32, (SUBLANES, tq), 0)
    cur = t_row // SEL_BLOCK
    n_blk_rows = n_sel // SUBLANES
    o_cmp, bias_t = [], []
    has_cmp = q0 + (lax.broadcasted_iota(jnp.int32, (rows, 1), 0) & (tq - 1)) >= CMP_BLOCK - 1
    for g in groups:
        s = per_head(cmp_valid, _nt_dot(qz[g], kcmp), NEG)
        m = jnp.max(s, axis=-1, keepdims=True)
        e = jnp.exp2(s - m)
        p = e * jnp.where(has_cmp, 1.0 / jnp.sum(e, axis=-1, keepdims=True), 0.0)
        o_cmp.append(jnp.dot(p.astype(BF16), vcmp, preferred_element_type=F32))
        p3 = p.reshape(HPG, tq, n_cmp)
        p_sum = p3[0]
        for h in range(1, HPG):
            p_sum = p_sum + p3[h]
        imp_t = lax.dot_general(ovt_ref[...], p_sum, (((1,), (1,)), ((), ())),
                                preferred_element_type=F32, precision=lax.Precision.HIGHEST)

        score = []
        for v in range(n_blk_rows):
            jj = sub + v * SUBLANES
            forced = (jj == 0) | (jj == cur) | (jj == cur - 1)
            blk_valid = jj * SEL_BLOCK <= t_row
            score.append(jnp.where(blk_valid,
                                   jnp.where(forced, FORCED_SCORE, imp_t[v * SUBLANES:(v + 1) * SUBLANES]), -1.0))
        rank = [jnp.zeros((SUBLANES, tq), jnp.int32)] * n_blk_rows
        for i in range(n_sel):
            vi, ri = divmod(i, SUBLANES)
            si = score[vi][ri:ri + 1, :]
            for v in range(n_blk_rows):
                ge = jnp.where(si >= score[v], 1, 0)
                gt = jnp.where(si > score[v], 1, 0)
                rank[v] = rank[v] + (ge if v > vi else gt if v < vi else jnp.where(sub > ri, ge, gt))
        blocks = [jnp.where(r >= top_n, NEG, 0.0) for r in rank]
        blocks += [jnp.zeros((SUBLANES, tq), F32)] * (HEAD_DIM // SUBLANES - n_blk_rows)
        bias_t.append(jnp.concatenate(blocks, axis=0))

    bias = jnp.concatenate([bias_t[1], bias_t[0]], axis=0).T.astype(BF16)
    bias4 = jnp.concatenate([bias] * HPG, axis=0)
    qb = [jnp.where(in_grp4[g], qz[g], bias4) for g in groups]

    a0 = pl.multiple_of(jnp.maximum(q0 - WINDOW, 0), tq)
    kpos = a0 + lax.broadcasted_iota(jnp.int32, (1, WINDOW + tq), 1)
    wmask = (kpos <= t_col) & (t_col - kpos < WINDOW)
    o_win = []
    for g in groups:
        s = per_head(wmask, _nt_dot(qz[g], kw_ref[0, pl.ds(a0, WINDOW + tq), :]), NEG)
        o_win.append(_softmax_step(s, vw_refs[g][0, pl.ds(a0, WINDOW + tq), :], *init)[1])

    def put_scores(start, slot):
        for g in groups:
            sbuf[slot, g] = _nt_dot(qb[g], ks_refs[g][0, pl.ds(start, TK_SEL), :])

    def consume(start, s, state):
        out = []
        for g in groups:
            out.extend(_softmax_step(s[g], vs_refs[g][0, pl.ds(start, TK_SEL), :], state[2 * g], state[2 * g + 1]))
        return tuple(out)

    def sel_step(start, slot, state):
        put_scores(start + TK_SEL, 1 - slot)
        return consume(start, [sbuf[slot, g] for g in groups], state)

    def sel_pair(kp, state):
        start = pl.multiple_of(kp * (2 * TK_SEL), 2 * TK_SEL)
        return sel_step(start + TK_SEL, 1, sel_step(start, 0, state))

    n_full = q0 // TK_SEL
    n_pair = n_full // 2
    put_scores(0, 0)
    state = lax.fori_loop(0, n_pair, sel_pair, init * N_KV)
    state = lax.fori_loop(
        n_pair * 2, n_full, lambda kt, st: sel_step(pl.multiple_of(kt * TK_SEL, TK_SEL), 0, st), state)
    last = pl.multiple_of(n_full * TK_SEL, TK_SEL)
    kpos = last + lax.broadcasted_iota(jnp.int32, (1, TK_SEL), 1)
    s_last = [per_head(kpos <= t_col, sbuf[n_full % 2, g], NEG) for g in groups]
    state = consume(last, s_last, state)
    o_sel = [state[2 * g + 1] for g in groups]

    for g in groups:
        heads = []
        for h in range(HPG):
            col = (HEAD_DIM if g == 0 else 0) + g * HPG + h
            r = slice(h * tq, (h + 1) * tq)
            w_sel = gates / o_sel[g][r]
            w_win = gates / o_win[g][r]
            heads.append(gates[:, col:col + 1] * o_cmp[g][r]
                         + w_sel[:, N_HEADS + col:N_HEADS + col + 1] * o_sel[g][r]
                         + w_win[:, 2 * N_HEADS + col:2 * N_HEADS + col + 1] * o_win[g][r])
        for c in range(HPG // 2):
            a, b = heads[2 * c], heads[2 * c + 1]
            if g == 0:
                pair = jnp.where(low, a, pltpu.roll(b, HEAD_DIM, 1))
            else:
                pair = jnp.where(low, pltpu.roll(a, HEAD_DIM, 1), b)
            oc = g * (HPG // 2) + c
            out_ref[0, :, oc * LANES:(oc + 1) * LANES] = pair.astype(out_ref.dtype)


def _attention(qpad, kcmp, vcmp, seq_kv, gates, ov_t):
    b, seq, _ = qpad.shape
    n_cmp = kcmp.shape[1]
    q_tile = lambda w: pl.BlockSpec((1, TQ, w), lambda i, j: (i, j, 0))
    per_b = lambda r: pl.BlockSpec((1, r, LANES), lambda i, j: (i, 0, 0))
    return pl.pallas_call(
        functools.partial(_attn_kernel, seq=seq),
        grid=(b, seq // TQ),
        in_specs=[q_tile(N_HEADS * LANES), per_b(n_cmp), per_b(n_cmp)] + [per_b(seq)] * len(seq_kv)
                 + [q_tile(LANES), _const_spec(ov_t.shape)],
        out_specs=q_tile(ATT_W),
        out_shape=jax.ShapeDtypeStruct((b, seq, ATT_W), BF16),
        scratch_shapes=[pltpu.VMEM((2, N_KV, HPG * TQ, TK_SEL), F32)],
        compiler_params=_cparams(("parallel", "parallel")),
        name="nsa_attention",
    )(qpad, kcmp, vcmp, *seq_kv, gates, ov_t)


def _mix_kernel(x_ref, g_ref, wgm_ref, attn_ref, dww_ref, dwb_ref, lng_ref, lnb_ref,
                wa_ref, wb_ref, wo_ref, h_ref, cbuf, carry, ybuf, pbuf):
    ts = x_ref.shape[1]
    head = (CONV_WIDTH - 1) * SUBLANES

    @pl.when(pl.program_id(1) == 0)
    def _():
        carry[...] = jnp.zeros_like(carry)

    x = x_ref[0]
    u = _rms(x, g_ref[...]).astype(BF16)
    o = 2 * CONV_CH
    zg = jnp.dot(u, wgm_ref[:, :o], preferred_element_type=F32)
    glu = _reorder(zg[:, :CONV_CH] * jax.nn.sigmoid(zg[:, CONV_CH:]), pbuf, True)

    attn = attn_ref[0]
    blk = 2 * LANES
    side = [functools.partial(lambda n0: jnp.dot(u, wgm_ref[:, o + n0:o + n0 + blk], preferred_element_type=F32), n0)
            for n0 in range(0, 2 * D_MODEL, blk)]
    side += [functools.partial(lambda n0: jnp.dot(attn, wa_ref[:, n0:n0 + blk], preferred_element_type=F32), n0)
             for n0 in range(0, D_MODEL, blk)]
    side_out = []

    tail = glu[ts - head:, :]
    cbuf[0:head, :] = _shift_in_groups(carry[...], tail)
    cbuf[head:head + ts, :] = glu
    carry[...] = tail
    sub_groups = CONV_ROWS // SUBLANES
    pace = jnp.zeros((SUBLANES, CONV_CH), F32)
    for i, r0 in enumerate(range(0, ts, CONV_ROWS)):
        acc = [dwb_ref[...] + pace] + [dwb_ref[...]] * (sub_groups - 1)
        wts = [dww_ref[k * SUBLANES:(k + 1) * SUBLANES, :] for k in range(CONV_WIDTH)]
        for m in range(CONV_WIDTH - 1 + sub_groups):
            grp = cbuf[r0 + m * SUBLANES:r0 + (m + 1) * SUBLANES, :]
            for a in range(sub_groups):
                if 0 <= m - a < CONV_WIDTH:
                    acc[a] = acc[a] + wts[m - a] * grp
        ybuf[r0:r0 + CONV_ROWS, :] = jnp.concatenate(acc, axis=0)
        if i < len(side):
            side_out.append(side[i]())
            pace = _dependent_zero(side_out[-1][ts - SUBLANES:, :], CONV_CH // blk)
    side_out += [f() for f in side[len(side_out):]]
    zm = jnp.concatenate(side_out[:2 * D_MODEL // blk], axis=1)
    y_a = jnp.concatenate(side_out[2 * D_MODEL // blk:], axis=1)

    y = _reorder(ybuf[...], pbuf, False)
    mu = jnp.mean(y, axis=-1, keepdims=True)
    var = jnp.mean(jnp.square(y - mu), axis=-1, keepdims=True)
    yn = (y - mu) * lax.rsqrt(var + EPS) * lng_ref[...] + lnb_ref[...]
    c = yn * jax.nn.sigmoid(yn)
    y_b = jnp.dot(c.astype(BF16), wb_ref[...], preferred_element_type=F32)
    mix = jax.nn.sigmoid(zm[:, :D_MODEL]) * y_a + jax.nn.sigmoid(zm[:, D_MODEL:]) * y_b
    h_ref[0] = x + jnp.dot(mix.astype(BF16), wo_ref[...], preferred_element_type=F32)


def _mix(x, g, wgm, attn, dww, dwb, lng, lnb, wa, wb, wo):
    b, seq, _ = x.shape
    ts = TS_MIX
    tile = lambda w: pl.BlockSpec((1, ts, w), lambda i, j: (i, j, 0))
    consts = [g, wgm]
    consts2 = [dww, dwb, lng, lnb, wa, wb, wo]
    return pl.pallas_call(
        _mix_kernel,
        grid=(b, seq // ts),
        in_specs=[tile(D_MODEL)] + [_const_spec(a.shape) for a in consts]
                 + [tile(ATT_W)] + [_const_spec(a.shape) for a in consts2],
        out_specs=tile(D_MODEL),
        out_shape=jax.ShapeDtypeStruct((b, seq, D_MODEL), F32),
        scratch_shapes=[pltpu.VMEM(((CONV_WIDTH - 1) * SUBLANES + ts, CONV_CH), F32),
                        pltpu.VMEM(((CONV_WIDTH - 1) * SUBLANES, CONV_CH), F32),
                        pltpu.VMEM((ts, CONV_CH), F32),
                        pltpu.VMEM((_perm_rows(ts, CONV_CH), LANES), F32)],
        compiler_params=_cparams(("arbitrary", "arbitrary")),
        name="conv_merge",
    )(x, g, wgm, attn, dww, dwb, lng, lnb, wa, wb, wo)


def _ffn_kernel(h_ref, gf_ref, wup_ref, fw_ref, fb_ref, wdn_ref, p_ref, gp_ref, wpg_ref, wpp_ref,
                gfin_ref, out_ref, ubuf, carry, actbuf, hbuf, pbuf, *, final_norm):
    ts = h_ref.shape[1]
    fc = FFN_CHUNK
    n_chunks = FFN_DIM // fc
    head = (FFN_CONV_WIDTH - 1) * SUBLANES

    @pl.when(pl.program_id(1) == 0)
    def _():
        carry[...] = jnp.zeros_like(carry)

    h = _reorder(h_ref[0], hbuf, True)
    u = _rms(h, gf_ref[...]).astype(BF16)

    def conv_half(c, col0, slot):
        cols = slice(col0 + c * fc, col0 + (c + 1) * fc)
        up = jnp.dot(u, wup_ref[:, cols], preferred_element_type=F32)
        tail = up[ts - head:, :]
        buf = ubuf.at[slot]
        buf[0:head, :] = _shift_in_groups(carry[:, cols], tail)
        buf[head:head + ts, :] = up
        carry[:, cols] = tail
        conv = fb_ref[:, cols]
        for k in range(FFN_CONV_WIDTH):
            conv = conv + fw_ref[k:k + 1, cols] * buf[k * SUBLANES:k * SUBLANES + ts, :]
        return conv

    h2 = h
    for c0 in range(0, n_chunks, FFN_DOWN_GROUP):
        c1 = min(c0 + FFN_DOWN_GROUP, n_chunks)
        for c in range(c0, c1):
            a = conv_half(c, 0, (2 * c) % 4)
            gt = conv_half(c, FFN_DIM, (2 * c + 1) % 4)
            actbuf[:, c * fc:(c + 1) * fc] = (gt * jax.nn.sigmoid(gt) * a).astype(BF16)
        h2 = h2 + jnp.dot(actbuf[:, c0 * fc:c1 * fc], wdn_ref[c0 * fc:c1 * fc, :], preferred_element_type=F32)

    u3 = _rms(h2, gp_ref[...]).astype(BF16)
    gate = jax.nn.sigmoid(jnp.dot(u3, wpg_ref[...], preferred_element_type=F32))
    proj = jnp.dot(_reorder(p_ref[0], pbuf, True).astype(BF16), wpp_ref[...], preferred_element_type=F32)
    h3 = h2 + gate * proj
    out_ref[0] = _reorder(_rms(h3, gfin_ref[...]) if final_norm else h3, hbuf, False)


def _ffn(h, gf, wup, fw, fb, wdn, p, gp, wpg, wpp, gfin, final_norm):
    b, seq, _ = h.shape
    ts = TS_FFN
    tile = lambda w: pl.BlockSpec((1, ts, w), lambda i, j: (i, j, 0))
    single = lambda a: _const_spec(a.shape)
    return pl.pallas_call(
        functools.partial(_ffn_kernel, final_norm=final_norm),
        grid=(b, seq // ts),
        in_specs=[tile(D_MODEL), single(gf), single(wup), single(fw), single(fb), single(wdn),
                  tile(PLE_DIM), single(gp), single(wpg), single(wpp), single(gfin)],
        out_specs=tile(D_MODEL),
        out_shape=jax.ShapeDtypeStruct((b, seq, D_MODEL), F32),
        scratch_shapes=[pltpu.VMEM((4, (FFN_CONV_WIDTH - 1) * SUBLANES + ts, FFN_CHUNK), F32),
                        pltpu.VMEM(((FFN_CONV_WIDTH - 1) * SUBLANES, 2 * FFN_DIM), F32),
                        pltpu.VMEM((ts, FFN_DIM), BF16),
                        pltpu.VMEM((_perm_rows(ts, D_MODEL), LANES), F32),
                        pltpu.VMEM((_perm_rows(ts, PLE_DIM), LANES), F32)],
        compiler_params=_cparams(("arbitrary", "arbitrary")),
        name="conv_ffn",
    )(h, gf, wup, fw, fb, wdn, p, gp, wpg, wpp, gfin)


def _expand_cmp_weights(w1, w2):
    half = CMP_BLOCK // 2
    w1r = w1.reshape(2, half, HEAD_DIM, CMP_HIDDEN).astype(BF16)
    w2b = w2.astype(BF16)
    z1, z2 = jnp.zeros_like(w1r), jnp.zeros_like(w2b)
    wab = jnp.stack([jnp.concatenate([w1r, z1], axis=-1), jnp.concatenate([z1, w1r], axis=-1)], axis=2)
    wab = wab.reshape(2, half * KV_W, N_KV * CMP_HIDDEN)
    w2e = jnp.concatenate([jnp.concatenate([w2b, z2], axis=-1), jnp.concatenate([z2, w2b], axis=-1)], axis=0)
    return wab[0], wab[1], w2e


def _expand_pe(pe):
    half = CMP_BLOCK // 2
    per = jnp.broadcast_to(pe.reshape(2, half, 1, HEAD_DIM), (2, half, N_KV, HEAD_DIM))
    return per.reshape(2, half * KV_W)


def kernel(x, p, positions, norm_mix_g, w_in, pe_k, pe_v, cmp_k_w1, cmp_k_w2, cmp_v_w1, cmp_v_w2, conv_dw_w, conv_dw_b, conv_ln_g, conv_ln_b, w_a, w_b, w_o, norm_ffn_g, w_up, ffn_dw_w, ffn_dw_b, w_down, norm_ple_g, w_ple_gate, w_ple_proj, norm_final_g):
    b, seq, _ = x.shape
    depth = w_in.shape[0]
    n_tok = b * seq
    n_cmp = seq // CMP_STRIDE
    n_sel = seq // SEL_BLOCK
    assert seq % TS_MIX == 0 and seq % TK_SEL == 0 and seq >= WINDOW and n_sel <= HEAD_DIM
    assert FFN_DIM % FFN_CHUNK == 0 and seq % TS_FFN == 0 and N_KV == 2

    half = ROT_DIM // 2
    inv_col = (ROPE_THETA ** (-jnp.arange(0, ROT_DIM, 2, dtype=F32) / ROT_DIM)).reshape(half, 1)
    cmp_start = np.arange(n_cmp) * CMP_STRIDE
    sel_start = np.arange(n_sel) * SEL_BLOCK
    overlap = np.maximum(np.minimum(cmp_start[:, None] + CMP_BLOCK, sel_start[None, :] + SEL_BLOCK)
                         - np.maximum(cmp_start[:, None], sel_start[None, :]), 0).astype(np.float32) / CMP_STRIDE
    ov_t = jnp.asarray(np.pad(overlap.T, ((0, HEAD_DIM - n_sel), (0, 0))))

    cmp_end = np.minimum(np.arange(n_cmp) * CMP_STRIDE + CMP_BLOCK - 1, seq - 1)
    pos_all = jnp.concatenate([positions.reshape(-1), positions[:, cmp_end].reshape(-1)])
    pos_all = jnp.pad(pos_all, (0, (-pos_all.shape[0]) % ROPE_TB)).reshape(1, -1)
    tab = _rope_tables(pos_all, inv_col)
    d = np.arange(LANES) % HEAD_DIM
    cos_t = tab[np.where(d < ROT_DIM, d % half, 3 * half)].T
    sin_t = tab[np.where(d < half, half + d, np.where(d < ROT_DIM, half + d, 3 * half + 1))].T
    assert n_tok % n_cmp == 0

    h = x
    row = lambda v: v.reshape(1, -1)
    for i in range(depth):
        w_qkv = jnp.concatenate(
            [w_in[i][:, :QKV_COLS + GATE_COLS],
             jnp.zeros((D_MODEL, LANES - GATE_COLS), w_in.dtype)], axis=1).astype(BF16)
        w_gm = w_in[i][:, QKV_COLS + GATE_COLS:].astype(BF16)

        qpad, kc, vc, *seq_kv, gates = _inproj(
            h.reshape(n_tok, D_MODEL), row(norm_mix_g[i]), w_qkv, cos_t, sin_t, seq)

        wka, wkb, wk2 = _expand_cmp_weights(cmp_k_w1[i], cmp_k_w2[i])
        wva, wvb, wv2 = _expand_cmp_weights(cmp_v_w1[i], cmp_v_w2[i])
        kcmp, vcmp = _compress(kc.reshape(b, seq, KV_W), vc.reshape(b, seq, KV_W),
                               _expand_pe(pe_k[i]), _expand_pe(pe_v[i]),
                               wka, wkb, wk2, wva, wvb, wv2, cos_t, sin_t, n_tok)

        s3 = lambda a: a.reshape(b, seq, a.shape[-1])
        attn = _attention(s3(qpad), kcmp, vcmp, [s3(a) for a in seq_kv], s3(gates), ov_t)

        h = _mix(h, row(norm_mix_g[i]), w_gm, attn,
                 jnp.repeat(conv_dw_w[i].reshape(CONV_WIDTH, CONV_CH), SUBLANES, axis=0),
                 jnp.broadcast_to(row(conv_dw_b[i]), (SUBLANES, CONV_CH)),
                 row(conv_ln_g[i]), row(conv_ln_b[i]),
                 w_a[i].astype(BF16), w_b[i].astype(BF16), w_o[i].astype(BF16))

        h = _ffn(h, row(norm_ffn_g[i]), w_up[i].astype(BF16),
                 ffn_dw_w[i].reshape(FFN_CONV_WIDTH, 2 * FFN_DIM), row(ffn_dw_b[i]),
                 w_down[i].astype(BF16), p[i], row(norm_ple_g[i]),
                 w_ple_gate[i].astype(BF16), w_ple_proj[i].astype(BF16), row(norm_final_g),
                 final_norm=(i == depth - 1))
    return h
```

```python
import functools

import numpy as np
import jax
import jax.numpy as jnp
from jax import lax
from jax.experimental import pallas as pl
from jax.experimental.pallas import tpu as pltpu

D_MODEL = 1024
PLE_DIM = 256
N_HEADS = 8
N_KV = 2
HPG = N_HEADS // N_KV
HEAD_DIM = 64
ROT_DIM = HEAD_DIM // 4
ROPE_THETA = 500000.0
CMP_BLOCK = 32
CMP_STRIDE = 16
CMP_HIDDEN = 128
SEL_BLOCK = 64
SEL_TOPK = 16
WINDOW = 512
CONV_CH = 512
CONV_WIDTH = 31
FFN_DIM = 2816
FFN_CONV_WIDTH = 3
EPS = 1e-6
NEG = -1e30
FORCED_SCORE = 1e6
LOG2_E = 1.4426950408889634

ATT_W = N_HEADS * HEAD_DIM
KV_W = N_KV * HEAD_DIM
QKV_COLS = ATT_W + 6 * KV_W
GATE_COLS = 3 * N_HEADS

LANES = 128
SUBLANES = 8
VMEM_LIMIT_BYTES = 56 * 1024 * 1024

TM_PROJ = 512
TQ = 128
TK_SEL = 512
TS_MIX = 512
TS_FFN = 512
FFN_CHUNK = 256
FFN_DOWN_GROUP = 4
CONV_ROWS = 32

F32 = jnp.float32
BF16 = jnp.bfloat16


def _cparams(sem):
    return pltpu.CompilerParams(dimension_semantics=sem, vmem_limit_bytes=VMEM_LIMIT_BYTES)


def _const_spec(shape):
    nd = len(shape)
    return pl.BlockSpec(shape, lambda *_: (0,) * nd, pipeline_mode=pl.Buffered(1))


def _rms(x, g):
    ms = jnp.mean(x * x, axis=-1, keepdims=True)
    return (x * lax.rsqrt(ms + EPS)) * g


def _rope(v, cos, sin_signed):
    d = lax.broadcasted_iota(jnp.int32, v.shape, 1) & (HEAD_DIM - 1)
    half = ROT_DIM // 2
    sw = jnp.where(d < half, pltpu.roll(v, LANES - half, 1), pltpu.roll(v, half, 1))
    return v * cos + sw * sin_signed


def _perm_rows(ts, w):
    return SUBLANES * ((ts // SUBLANES) * (w // LANES) + 4)


def _perm_access(ts, w):
    n, wc = ts // SUBLANES, w // LANES
    pitch = n * wc + 4
    natural = lambda t0, c: pl.ds((t0 // n) * pitch + (t0 % n) * wc + c, SUBLANES, stride=wc)
    strided = lambda j, c: pl.ds(j * wc + c, SUBLANES, stride=pitch)
    return n, wc, natural, strided


def _reorder(val, pbuf, to_strided):
    ts, w = val.shape
    n, wc, natural, strided = _perm_access(ts, w)
    src, dst = (natural, strided) if to_strided else (strided, natural)
    for g in range(n):
        for c in range(wc):
            pbuf[src(g * SUBLANES, c) if to_strided else src(g, c), :] = (
                val[g * SUBLANES:(g + 1) * SUBLANES, c * LANES:(c + 1) * LANES])
    return jnp.concatenate(
        [jnp.concatenate([pbuf[dst(g, c) if to_strided else dst(g * SUBLANES, c), :] for c in range(wc)], axis=1)
         for g in range(n)], axis=0)


def _dependent_zero(x, reps):
    bits = pltpu.bitcast(x, jnp.uint32)
    zero = pltpu.bitcast(lax.shift_right_logical(lax.shift_right_logical(bits, jnp.uint32(16)), jnp.uint32(16)), F32)
    return jnp.concatenate([zero] * reps, axis=1)


def _shift_in_groups(prev_tail, tail):
    sub = lax.broadcasted_iota(jnp.int32, tail.shape, 0) & (SUBLANES - 1)
    merged = jnp.where(sub == SUBLANES - 1, prev_tail, tail)
    return jnp.concatenate(
        [pltpu.roll(merged[i:i + SUBLANES], 1, 0) for i in range(0, tail.shape[0], SUBLANES)], axis=0)


ROPE_TB = 2048
ROPE_ROWS = 4 * (ROT_DIM // 2)


def _rope_table_kernel(pos_ref, inv_ref, tab_ref):
    half = ROT_DIM // 2
    ang = pos_ref[...].astype(F32) * inv_ref[...]
    sin = jnp.sin(ang)
    tab_ref[0:half, :] = jnp.cos(ang)
    tab_ref[half:2 * half, :] = -sin
    tab_ref[2 * half:3 * half, :] = sin
    sub = lax.broadcasted_iota(jnp.int32, (half, ang.shape[1]), 0)
    tab_ref[3 * half:, :] = jnp.where(sub == 0, 1.0, 0.0)


def _rope_tables(pos_row, inv_col):
    n = pos_row.shape[1]
    return pl.pallas_call(
        _rope_table_kernel,
        grid=(n // ROPE_TB,),
        in_specs=[pl.BlockSpec((1, ROPE_TB), lambda i: (0, i)), _const_spec(inv_col.shape)],
        out_specs=pl.BlockSpec((ROPE_ROWS, ROPE_TB), lambda i: (0, i)),
        out_shape=jax.ShapeDtypeStruct((ROPE_ROWS, n), F32),
        compiler_params=_cparams(("parallel",)),
        name="rope_tables",
    )(pos_row, inv_col)


def _inproj_kernel(x_ref, g_ref, w_ref, cos_ref, sin_ref,
                   qpad_ref, kc_ref, vc_ref, ks0_ref, ks1_ref, vs0_ref, vs1_ref, kw_ref, vw0_ref, vw1_ref,
                   gate_ref, *, seq):
    tm = x_ref.shape[0]
    u = _rms(x_ref[...], g_ref[...])
    z = jnp.dot(u.astype(BF16), w_ref[...], preferred_element_type=F32)
    cos = cos_ref[...]
    sin = sin_ref[...]
    lane = lax.broadcasted_iota(jnp.int32, (tm, LANES), 1)
    low = lane < HEAD_DIM

    scale = HEAD_DIM ** -0.5 * LOG2_E
    for c in range(ATT_W // LANES):
        qc = _rope(z[:, c * LANES:(c + 1) * LANES], cos, sin) * scale
        qsw = pltpu.roll(qc, HEAD_DIM, 1)
        for e in range(2):
            hh = 2 * c + e
            grp = hh // HPG
            src = qc if e == grp else qsw
            keep = low if grp == 0 else jnp.logical_not(low)
            qpad_ref[:, hh * LANES:(hh + 1) * LANES] = jnp.where(keep, src, 0.0).astype(BF16)

    o = ATT_W
    kc_ref[...] = z[:, o:o + LANES]
    vc_ref[...] = z[:, o + LANES:o + 2 * LANES]
    ks = _rope(z[:, o + 2 * LANES:o + 3 * LANES], cos, sin)
    kw_ref[...] = _rope(z[:, o + 4 * LANES:o + 5 * LANES], cos, sin).astype(BF16)
    sig = jax.nn.sigmoid(z[:, o + 6 * LANES:o + 7 * LANES])
    gate_ref[...] = jnp.where(low, sig, pltpu.roll(sig, HEAD_DIM, 1))
    for v, (r0, r1) in ((z[:, o + 3 * LANES:o + 4 * LANES], (vs0_ref, vs1_ref)),
                        (z[:, o + 5 * LANES:o + 6 * LANES], (vw0_ref, vw1_ref))):
        r0[...] = jnp.where(low, v, 1.0).astype(BF16)
        r1[...] = jnp.where(low, 1.0, v).astype(BF16)

    s0 = (pl.program_id(0) * tm) % seq
    blk = (s0 + lax.broadcasted_iota(jnp.int32, (tm, LANES), 0)) // SEL_BLOCK
    hot0 = (lane - HEAD_DIM == blk).astype(F32)
    hot1 = (lane == blk).astype(F32)
    ks0_ref[...] = jnp.where(low, ks, hot0).astype(BF16)
    ks1_ref[...] = jnp.where(low, hot1, ks).astype(BF16)


def _inproj(x2, g, w_qkv, cos_t, sin_t, seq):
    n = x2.shape[0]
    tm = TM_PROJ
    ncols = w_qkv.shape[1]
    tok = lambda w: pl.BlockSpec((tm, w), lambda i: (i, 0))
    outs = [(N_HEADS * LANES, BF16), (LANES, F32), (LANES, F32)] + [(LANES, BF16)] * 7 + [(LANES, F32)]
    return pl.pallas_call(
        functools.partial(_inproj_kernel, seq=seq),
        grid=(n // tm,),
        in_specs=[tok(D_MODEL), _const_spec((1, D_MODEL)), _const_spec((D_MODEL, ncols)),
                  tok(LANES), tok(LANES)],
        out_specs=[tok(w) for w, _ in outs],
        out_shape=[jax.ShapeDtypeStruct((n, w), dt) for w, dt in outs],
        compiler_params=_cparams(("parallel",)),
        name="in_proj",
    )(x2, g, w_qkv, cos_t, sin_t)


def _compress_kernel(kx_ref, vx_ref, pek_ref, pev_ref, wka_ref, wkb_ref, wk2_ref,
                     wva_ref, wvb_ref, wv2_ref, cos_ref, sin_ref, kcmp_ref, vcmp_ref):
    def comp(x, pe_ref, wa_ref, wb_ref, w2_ref):
        n = x.shape[0]
        xa = (x + pe_ref[0:1, :]).astype(BF16)
        xb = (x + pe_ref[1:2, :]).astype(BF16)
        ya = jnp.dot(xa, wa_ref[...], preferred_element_type=F32)
        yb = jnp.dot(xb, wb_ref[...], preferred_element_type=F32)
        h = ya + pltpu.roll(yb, n - 1, 0)
        return jnp.dot(jax.nn.gelu(h).astype(BF16), w2_ref[...], preferred_element_type=F32)

    def stride_rows(ref):
        n = ref.shape[1] // CMP_STRIDE
        return jnp.concatenate([ref[0, pl.ds(r, n, stride=CMP_STRIDE), :] for r in range(CMP_STRIDE)], axis=1)

    kc = comp(stride_rows(kx_ref), pek_ref, wka_ref, wkb_ref, wk2_ref)
    kcmp_ref[0] = _rope(kc, cos_ref[...], sin_ref[...]).astype(BF16)
    vcmp_ref[0] = comp(stride_rows(vx_ref), pev_ref, wva_ref, wvb_ref, wv2_ref).astype(BF16)


def _compress(kx, vx, pek, pev, wka, wkb, wk2, wva, wvb, wv2, cos_t, sin_t, row0):
    b, seq, w = kx.shape
    n = seq // CMP_STRIDE
    per_b = lambda s: pl.BlockSpec((1,) + s, lambda i: (i, 0, 0))
    table = pl.BlockSpec((n, LANES), lambda i: (row0 // n + i, 0))
    return pl.pallas_call(
        _compress_kernel,
        grid=(b,),
        in_specs=[per_b((seq, w)), per_b((seq, w)), _const_spec(pek.shape), _const_spec(pev.shape),
                  _const_spec(wka.shape), _const_spec(wkb.shape), _const_spec(wk2.shape),
                  _const_spec(wva.shape), _const_spec(wvb.shape), _const_spec(wv2.shape),
                  table, table],
        out_specs=[per_b((n, LANES))] * 2,
        out_shape=[jax.ShapeDtypeStruct((b, n, LANES), BF16)] * 2,
        compiler_params=_cparams(("parallel",)),
        name="kv_compress",
    )(kx, vx, pek, pev, wka, wkb, wk2, wva, wvb, wv2, cos_t, sin_t)


def _nt_dot(a, b):
    return lax.dot_general(a, b, (((1,), (1,)), ((), ())), preferred_element_type=F32)


def _softmax_step(s, v, m, acc):
    m_new = jnp.maximum(m, jnp.max(s, axis=-1, keepdims=True))
    p = jnp.exp2(s - m_new)
    acc_new = jnp.exp2(m - m_new) * acc + jnp.dot(p.astype(BF16), v, preferred_element_type=F32)
    return m_new, acc_new


def _attn_kernel(q_ref, kcmp_ref, vcmp_ref, ks0_ref, ks1_ref, vs0_ref, vs1_ref, kw_ref, vw0_ref, vw1_ref,
                 gate_ref, ovt_ref, out_ref, sbuf, *, seq, n_full):
    tq = q_ref.shape[1]
    n_cmp = kcmp_ref.shape[1]
    n_sel = seq // SEL_BLOCK
    top_n = min(SEL_TOPK, n_sel)
    rows = HPG * tq
    groups = range(N_KV)
    q0 = pl.multiple_of(n_full * TK_SEL + pl.program_id(1) * tq, tq)
    t_col = q0 + lax.broadcasted_iota(jnp.int32, (tq, 1), 0)
    t_row = q0 + lax.broadcasted_iota(jnp.int32, (1, tq), 1)
    low = lax.broadcasted_iota(jnp.int32, (tq, LANES), 1) < HEAD_DIM
    low4 = lax.broadcasted_iota(jnp.int32, (rows, LANES), 1) < HEAD_DIM
    in_grp4 = (low4, jnp.logical_not(low4))
    ks_refs, vs_refs, vw_refs = (ks0_ref, ks1_ref), (vs0_ref, vs1_ref), (vw0_ref, vw1_ref)

    def per_head(mask, s, fill):
        s3 = s.reshape(HPG, tq, s.shape[-1])
        return jnp.where(mask[None], s3, fill).reshape(s.shape)

    cmp_end = lax.broadcasted_iota(jnp.int32, (1, n_cmp), 1) * CMP_STRIDE + (CMP_BLOCK - 1)
    cmp_valid = cmp_end <= t_col
    kcmp = kcmp_ref[0]
    vcmp = vcmp_ref[0]
    gates = gate_ref[0]
    qz = [jnp.concatenate(
        [q_ref[0, :, (g * HPG + h) * LANES:(g * HPG + h + 1) * LANES] for h in range(HPG)], axis=0)
        for g in groups]

    init = (jnp.full((rows, 1), NEG, F32), jnp.zeros((rows, LANES), F32))

    sub = lax.broadcasted_iota(jnp.int32, (SUBLANES, tq), 0)
    cur = t_row // SEL_BLOCK
    n_blk_rows = n_sel // SUBLANES
    o_cmp, bias_t = [], []
    has_cmp = q0 + (lax.broadcasted_iota(jnp.int32, (rows, 1), 0) & (tq - 1)) >= CMP_BLOCK - 1
    for g in groups:
        s = per_head(cmp_valid, _nt_dot(qz[g], kcmp), NEG)
        m = jnp.max(s, axis=-1, keepdims=True)
        e = jnp.exp2(s - m)
        p = e * jnp.where(has_cmp, 1.0 / jnp.sum(e, axis=-1, keepdims=True), 0.0)
        o_cmp.append(jnp.dot(p.astype(BF16), vcmp, preferred_element_type=F32))
        p3 = p.reshape(HPG, tq, n_cmp)
        p_sum = p3[0]
        for h in range(1, HPG):
            p_sum = p_sum + p3[h]
        imp_t = lax.dot_general(ovt_ref[...], p_sum, (((1,), (1,)), ((), ())),
                                preferred_element_type=F32, precision=lax.Precision.HIGHEST)

        score = []
        for v in range(n_blk_rows):
            jj = sub + v * SUBLANES
            forced = (jj == 0) | (jj == cur) | (jj == cur - 1)
            blk_valid = jj * SEL_BLOCK <= t_row
            score.append(jnp.where(blk_valid,
                                   jnp.where(forced, FORCED_SCORE, imp_t[v * SUBLANES:(v + 1) * SUBLANES]), -1.0))
        rank = [jnp.zeros((SUBLANES, tq), jnp.int32)] * n_blk_rows
        for i in range(n_sel):
            vi, ri = divmod(i, SUBLANES)
            si = score[vi][ri:ri + 1, :]
            for v in range(n_blk_rows):
                ge = jnp.where(si >= score[v], 1, 0)
                gt = jnp.where(si > score[v], 1, 0)
                rank[v] = rank[v] + (ge if v > vi else gt if v < vi else jnp.where(sub > ri, ge, gt))
        blocks = [jnp.where(r >= top_n, NEG, 0.0) for r in rank]
        blocks += [jnp.zeros((SUBLANES, tq), F32)] * (HEAD_DIM // SUBLANES - n_blk_rows)
        bias_t.append(jnp.concatenate(blocks, axis=0))

    bias = jnp.concatenate([bias_t[1], bias_t[0]], axis=0).T.astype(BF16)
    bias4 = jnp.concatenate([bias] * HPG, axis=0)
    qb = [jnp.where(in_grp4[g], qz[g], bias4) for g in groups]

    a0 = pl.multiple_of(jnp.maximum(q0 - WINDOW, 0), tq)
    kpos = a0 + lax.broadcasted_iota(jnp.int32, (1, WINDOW + tq), 1)
    wmask = (kpos <= t_col) & (t_col - kpos < WINDOW)
    o_win = []
    for g in groups:
        s = per_head(wmask, _nt_dot(qz[g], kw_ref[0, pl.ds(a0, WINDOW + tq), :]), NEG)
        o_win.append(_softmax_step(s, vw_refs[g][0, pl.ds(a0, WINDOW + tq), :], *init)[1])

    def put_scores(start, slot):
        for g in groups:
            sbuf[slot, g] = _nt_dot(qb[g], ks_refs[g][0, pl.ds(start, TK_SEL), :])

    def consume(start, s, state):
        out = []
        for g in groups:
            out.extend(_softmax_step(s[g], vs_refs[g][0, pl.ds(start, TK_SEL), :], state[2 * g], state[2 * g + 1]))
        return tuple(out)

    def sel_step(start, slot, state):
        put_scores(start + TK_SEL, 1 - slot)
        return consume(start, [sbuf[slot, g] for g in groups], state)

    put_scores(0, 0)
    state = init * N_KV
    for kt in range(n_full):
        state = sel_step(kt * TK_SEL, kt % 2, state)
    last = n_full * TK_SEL
    kpos = last + lax.broadcasted_iota(jnp.int32, (1, TK_SEL), 1)
    s_last = [per_head(kpos <= t_col, sbuf[n_full % 2, g], NEG) for g in groups]
    state = consume(last, s_last, state)
    o_sel = [state[2 * g + 1] for g in groups]

    for g in groups:
        heads = []
        for h in range(HPG):
            col = (HEAD_DIM if g == 0 else 0) + g * HPG + h
            r = slice(h * tq, (h + 1) * tq)
            w_sel = gates / o_sel[g][r]
            w_win = gates / o_win[g][r]
            heads.append(gates[:, col:col + 1] * o_cmp[g][r]
                         + w_sel[:, N_HEADS + col:N_HEADS + col + 1] * o_sel[g][r]
                         + w_win[:, 2 * N_HEADS + col:2 * N_HEADS + col + 1] * o_win[g][r])
        for c in range(HPG // 2):
            a, b = heads[2 * c], heads[2 * c + 1]
            if g == 0:
                pair = jnp.where(low, a, pltpu.roll(b, HEAD_DIM, 1))
            else:
                pair = jnp.where(low, pltpu.roll(a, HEAD_DIM, 1), b)
            oc = g * (HPG // 2) + c
            out_ref[0, :, oc * LANES:(oc + 1) * LANES] = pair.astype(out_ref.dtype)


def _attention(qpad, kcmp, vcmp, seq_kv, gates, ov_t):
    b, seq, _ = qpad.shape
    n_cmp = kcmp.shape[1]
    per_b = lambda r: pl.BlockSpec((1, r, LANES), lambda i, j: (i, 0, 0))
    per_group = TK_SEL // TQ
    outs = []
    for n_full in range(seq // TK_SEL):
        q_tile = lambda w, n=n_full: pl.BlockSpec((1, TQ, w), lambda i, j: (i, n * per_group + j, 0))
        outs.append(pl.pallas_call(
            functools.partial(_attn_kernel, seq=seq, n_full=n_full),
            grid=(b, per_group),
            in_specs=[q_tile(N_HEADS * LANES), per_b(n_cmp), per_b(n_cmp)] + [per_b(seq)] * len(seq_kv)
                     + [q_tile(LANES), _const_spec(ov_t.shape)],
            out_specs=pl.BlockSpec((1, TQ, ATT_W), lambda i, j: (i, j, 0)),
            out_shape=jax.ShapeDtypeStruct((b, TK_SEL, ATT_W), BF16),
            scratch_shapes=[pltpu.VMEM((2, N_KV, HPG * TQ, TK_SEL), F32)],
            compiler_params=_cparams(("parallel", "parallel")),
            name=f"nsa_attention_{n_full}",
        )(qpad, kcmp, vcmp, *seq_kv, gates, ov_t))
    return jnp.concatenate(outs, axis=1)


def _mix_kernel(x_ref, g_ref, wgm_ref, attn_ref, dww_ref, dwb_ref, lng_ref, lnb_ref,
                wa_ref, wb_ref, wo_ref, h_ref, cbuf, carry, ybuf, pbuf):
    ts = x_ref.shape[1]
    head = (CONV_WIDTH - 1) * SUBLANES

    @pl.when(pl.program_id(1) == 0)
    def _():
        carry[...] = jnp.zeros_like(carry)

    x = x_ref[0]
    u = _rms(x, g_ref[...]).astype(BF16)
    o = 2 * CONV_CH
    zg = jnp.dot(u, wgm_ref[:, :o], preferred_element_type=F32)
    glu = _reorder(zg[:, :CONV_CH] * jax.nn.sigmoid(zg[:, CONV_CH:]), pbuf, True)

    attn = attn_ref[0]
    blk = 2 * LANES
    side = [functools.partial(lambda n0: jnp.dot(u, wgm_ref[:, o + n0:o + n0 + blk], preferred_element_type=F32), n0)
            for n0 in range(0, 2 * D_MODEL, blk)]
    side += [functools.partial(lambda n0: jnp.dot(attn, wa_ref[:, n0:n0 + blk], preferred_element_type=F32), n0)
             for n0 in range(0, D_MODEL, blk)]
    side_out = []

    tail = glu[ts - head:, :]
    cbuf[0:head, :] = _shift_in_groups(carry[...], tail)
    cbuf[head:head + ts, :] = glu
    carry[...] = tail
    sub_groups = CONV_ROWS // SUBLANES
    pace = jnp.zeros((SUBLANES, CONV_CH), F32)
    for i, r0 in enumerate(range(0, ts, CONV_ROWS)):
        acc = [dwb_ref[...] + pace] + [dwb_ref[...]] * (sub_groups - 1)
        wts = [dww_ref[k * SUBLANES:(k + 1) * SUBLANES, :] for k in range(CONV_WIDTH)]
        for m in range(CONV_WIDTH - 1 + sub_groups):
            grp = cbuf[r0 + m * SUBLANES:r0 + (m + 1) * SUBLANES, :]
            for a in range(sub_groups):
                if 0 <= m - a < CONV_WIDTH:
                    acc[a] = acc[a] + wts[m - a] * grp
        ybuf[r0:r0 + CONV_ROWS, :] = jnp.concatenate(acc, axis=0)
        if i < len(side):
            side_out.append(side[i]())
            pace = _dependent_zero(side_out[-1][ts - SUBLANES:, :], CONV_CH // blk)
    side_out += [f() for f in side[len(side_out):]]
    zm = jnp.concatenate(side_out[:2 * D_MODEL // blk], axis=1)
    y_a = jnp.concatenate(side_out[2 * D_MODEL // blk:], axis=1)

    y = _reorder(ybuf[...], pbuf, False)
    mu = jnp.mean(y, axis=-1, keepdims=True)
    var = jnp.mean(jnp.square(y - mu), axis=-1, keepdims=True)
    yn = (y - mu) * lax.rsqrt(var + EPS) * lng_ref[...] + lnb_ref[...]
    c = yn * jax.nn.sigmoid(yn)
    y_b = jnp.dot(c.astype(BF16), wb_ref[...], preferred_element_type=F32)
    mix = jax.nn.sigmoid(zm[:, :D_MODEL]) * y_a + jax.nn.sigmoid(zm[:, D_MODEL:]) * y_b
    h_ref[0] = x + jnp.dot(mix.astype(BF16), wo_ref[...], preferred_element_type=F32)


def _mix(x, g, wgm, attn, dww, dwb, lng, lnb, wa, wb, wo):
    b, seq, _ = x.shape
    ts = TS_MIX
    tile = lambda w: pl.BlockSpec((1, ts, w), lambda i, j: (i, j, 0))
    consts = [g, wgm]
    consts2 = [dww, dwb, lng, lnb, wa, wb, wo]
    return pl.pallas_call(
        _mix_kernel,
        grid=(b, seq // ts),
        in_specs=[tile(D_MODEL)] + [_const_spec(a.shape) for a in consts]
                 + [tile(ATT_W)] + [_const_spec(a.shape) for a in consts2],
        out_specs=tile(D_MODEL),
        out_shape=jax.ShapeDtypeStruct((b, seq, D_MODEL), F32),
        scratch_shapes=[pltpu.VMEM(((CONV_WIDTH - 1) * SUBLANES + ts, CONV_CH), F32),
                        pltpu.VMEM(((CONV_WIDTH - 1) * SUBLANES, CONV_CH), F32),
                        pltpu.VMEM((ts, CONV_CH), F32),
                        pltpu.VMEM((_perm_rows(ts, CONV_CH), LANES), F32)],
        compiler_params=_cparams(("arbitrary", "arbitrary")),
        name="conv_merge",
    )(x, g, wgm, attn, dww, dwb, lng, lnb, wa, wb, wo)


def _ffn_kernel(h_ref, gf_ref, wup_ref, fw_ref, fb_ref, wdn_ref, p_ref, gp_ref, wpg_ref, wpp_ref,
                gfin_ref, out_ref, ubuf, carry, actbuf, hbuf, pbuf, *, final_norm):
    ts = h_ref.shape[1]
    fc = FFN_CHUNK
    n_chunks = FFN_DIM // fc
    head = (FFN_CONV_WIDTH - 1) * SUBLANES

    @pl.when(pl.program_id(1) == 0)
    def _():
        carry[...] = jnp.zeros_like(carry)

    h = _reorder(h_ref[0], hbuf, True)
    u = _rms(h, gf_ref[...]).astype(BF16)

    def conv_half(c, col0, slot):
        cols = slice(col0 + c * fc, col0 + (c + 1) * fc)
        up = jnp.dot(u, wup_ref[:, cols], preferred_element_type=F32)
        tail = up[ts - head:, :]
        buf = ubuf.at[slot]
        buf[0:head, :] = _shift_in_groups(carry[:, cols], tail)
        buf[head:head + ts, :] = up
        carry[:, cols] = tail
        conv = fb_ref[:, cols]
        for k in range(FFN_CONV_WIDTH):
            conv = conv + fw_ref[k:k + 1, cols] * buf[k * SUBLANES:k * SUBLANES + ts, :]
        return conv

    h2 = h
    for c0 in range(0, n_chunks, FFN_DOWN_GROUP):
        c1 = min(c0 + FFN_DOWN_GROUP, n_chunks)
        for c in range(c0, c1):
            a = conv_half(c, 0, (2 * c) % 4)
            gt = conv_half(c, FFN_DIM, (2 * c + 1) % 4)
            actbuf[:, c * fc:(c + 1) * fc] = (gt * jax.nn.sigmoid(gt) * a).astype(BF16)
        h2 = h2 + jnp.dot(actbuf[:, c0 * fc:c1 * fc], wdn_ref[c0 * fc:c1 * fc, :], preferred_element_type=F32)

    u3 = _rms(h2, gp_ref[...]).astype(BF16)
    gate = jax.nn.sigmoid(jnp.dot(u3, wpg_ref[...], preferred_element_type=F32))
    proj = jnp.dot(_reorder(p_ref[0], pbuf, True).astype(BF16), wpp_ref[...], preferred_element_type=F32)
    h3 = h2 + gate * proj
    out_ref[0] = _reorder(_rms(h3, gfin_ref[...]) if final_norm else h3, hbuf, False)


def _ffn(h, gf, wup, fw, fb, wdn, p, gp, wpg, wpp, gfin, final_norm):
    b, seq, _ = h.shape
    ts = TS_FFN
    tile = lambda w: pl.BlockSpec((1, ts, w), lambda i, j: (i, j, 0))
    single = lambda a: _const_spec(a.shape)
    return pl.pallas_call(
        functools.partial(_ffn_kernel, final_norm=final_norm),
        grid=(b, seq // ts),
        in_specs=[tile(D_MODEL), single(gf), single(wup), single(fw), single(fb), single(wdn),
                  tile(PLE_DIM), single(gp), single(wpg), single(wpp), single(gfin)],
        out_specs=tile(D_MODEL),
        out_shape=jax.ShapeDtypeStruct((b, seq, D_MODEL), F32),
        scratch_shapes=[pltpu.VMEM((4, (FFN_CONV_WIDTH - 1) * SUBLANES + ts, FFN_CHUNK), F32),
                        pltpu.VMEM(((FFN_CONV_WIDTH - 1) * SUBLANES, 2 * FFN_DIM), F32),
                        pltpu.VMEM((ts, FFN_DIM), BF16),
                        pltpu.VMEM((_perm_rows(ts, D_MODEL), LANES), F32),
                        pltpu.VMEM((_perm_rows(ts, PLE_DIM), LANES), F32)],
        compiler_params=_cparams(("arbitrary", "arbitrary")),
        name="conv_ffn",
    )(h, gf, wup, fw, fb, wdn, p, gp, wpg, wpp, gfin)


def _expand_cmp_weights(w1, w2):
    half = CMP_BLOCK // 2
    w1r = w1.reshape(2, half, HEAD_DIM, CMP_HIDDEN).astype(BF16)
    w2b = w2.astype(BF16)
    z1, z2 = jnp.zeros_like(w1r), jnp.zeros_like(w2b)
    wab = jnp.stack([jnp.concatenate([w1r, z1], axis=-1), jnp.concatenate([z1, w1r], axis=-1)], axis=2)
    wab = wab.reshape(2, half * KV_W, N_KV * CMP_HIDDEN)
    w2e = jnp.concatenate([jnp.concatenate([w2b, z2], axis=-1), jnp.concatenate([z2, w2b], axis=-1)], axis=0)
    return wab[0], wab[1], w2e


def _expand_pe(pe):
    half = CMP_BLOCK // 2
    per = jnp.broadcast_to(pe.reshape(2, half, 1, HEAD_DIM), (2, half, N_KV, HEAD_DIM))
    return per.reshape(2, half * KV_W)


def kernel(x, p, positions, norm_mix_g, w_in, pe_k, pe_v, cmp_k_w1, cmp_k_w2, cmp_v_w1, cmp_v_w2, conv_dw_w, conv_dw_b, conv_ln_g, conv_ln_b, w_a, w_b, w_o, norm_ffn_g, w_up, ffn_dw_w, ffn_dw_b, w_down, norm_ple_g, w_ple_gate, w_ple_proj, norm_final_g):
    b, seq, _ = x.shape
    depth = w_in.shape[0]
    n_tok = b * seq
    n_cmp = seq // CMP_STRIDE
    n_sel = seq // SEL_BLOCK
    assert seq % TS_MIX == 0 and seq % TK_SEL == 0 and seq >= WINDOW and n_sel <= HEAD_DIM
    assert FFN_DIM % FFN_CHUNK == 0 and seq % TS_FFN == 0 and N_KV == 2

    half = ROT_DIM // 2
    inv_col = (ROPE_THETA ** (-jnp.arange(0, ROT_DIM, 2, dtype=F32) / ROT_DIM)).reshape(half, 1)
    cmp_start = np.arange(n_cmp) * CMP_STRIDE
    sel_start = np.arange(n_sel) * SEL_BLOCK
    overlap = np.maximum(np.minimum(cmp_start[:, None] + CMP_BLOCK, sel_start[None, :] + SEL_BLOCK)
                         - np.maximum(cmp_start[:, None], sel_start[None, :]), 0).astype(np.float32) / CMP_STRIDE
    ov_t = jnp.asarray(np.pad(overlap.T, ((0, HEAD_DIM - n_sel), (0, 0))))

    cmp_end = np.minimum(np.arange(n_cmp) * CMP_STRIDE + CMP_BLOCK - 1, seq - 1)
    pos_all = jnp.concatenate([positions.reshape(-1), positions[:, cmp_end].reshape(-1)])
    pos_all = jnp.pad(pos_all, (0, (-pos_all.shape[0]) % ROPE_TB)).reshape(1, -1)
    tab = _rope_tables(pos_all, inv_col)
    d = np.arange(LANES) % HEAD_DIM
    cos_t = tab[np.where(d < ROT_DIM, d % half, 3 * half)].T
    sin_t = tab[np.where(d < half, half + d, np.where(d < ROT_DIM, half + d, 3 * half + 1))].T
    assert n_tok % n_cmp == 0

    h = x
    row = lambda v: v.reshape(1, -1)
    for i in range(depth):
        w_qkv = jnp.concatenate(
            [w_in[i][:, :QKV_COLS + GATE_COLS],
             jnp.zeros((D_MODEL, LANES - GATE_COLS), w_in.dtype)], axis=1).astype(BF16)
        w_gm = w_in[i][:, QKV_COLS + GATE_COLS:].astype(BF16)

        qpad, kc, vc, *seq_kv, gates = _inproj(
            h.reshape(n_tok, D_MODEL), row(norm_mix_g[i]), w_qkv, cos_t, sin_t, seq)

        wka, wkb, wk2 = _expand_cmp_weights(cmp_k_w1[i], cmp_k_w2[i])
        wva, wvb, wv2 = _expand_cmp_weights(cmp_v_w1[i], cmp_v_w2[i])
        kcmp, vcmp = _compress(kc.reshape(b, seq, KV_W), vc.reshape(b, seq, KV_W),
                               _expand_pe(pe_k[i]), _expand_pe(pe_v[i]),
                               wka, wkb, wk2, wva, wvb, wv2, cos_t, sin_t, n_tok)

        s3 = lambda a: a.reshape(b, seq, a.shape[-1])
        attn = _attention(s3(qpad), kcmp, vcmp, [s3(a) for a in seq_kv], s3(gates), ov_t)

        h = _mix(h, row(norm_mix_g[i]), w_gm, attn,
                 jnp.repeat(conv_dw_w[i].reshape(CONV_WIDTH, CONV_CH), SUBLANES, axis=0),
                 jnp.broadcast_to(row(conv_dw_b[i]), (SUBLANES, CONV_CH)),
                 row(conv_ln_g[i]), row(conv_ln_b[i]),
                 w_a[i].astype(BF16), w_b[i].astype(BF16), w_o[i].astype(BF16))

        h = _ffn(h, row(norm_ffn_g[i]), w_up[i].astype(BF16),
                 ffn_dw_w[i].reshape(FFN_CONV_WIDTH, 2 * FFN_DIM), row(ffn_dw_b[i]),
                 w_down[i].astype(BF16), p[i], row(norm_ple_g[i]),
                 w_ple_gate[i].astype(BF16), w_ple_proj[i].astype(BF16), row(norm_final_g),
                 final_norm=(i == depth - 1))
    return h
```

```python
import functools

import numpy as np
import jax
import jax.numpy as jnp
from jax import lax
from jax.experimental import pallas as pl
from jax.experimental.pallas import tpu as pltpu

D_MODEL = 1024
PLE_DIM = 256
N_HEADS = 8
N_KV = 2
HPG = N_HEADS // N_KV
HEAD_DIM = 64
ROT_DIM = HEAD_DIM // 4
ROPE_THETA = 500000.0
CMP_BLOCK = 32
CMP_STRIDE = 16
CMP_HIDDEN = 128
SEL_BLOCK = 64
SEL_TOPK = 16
WINDOW = 512
CONV_CH = 512
CONV_WIDTH = 31
FFN_DIM = 2816
FFN_CONV_WIDTH = 3
EPS = 1e-6
NEG = -1e30
FORCED_SCORE = 1e6
LOG2_E = 1.4426950408889634

ATT_W = N_HEADS * HEAD_DIM
KV_W = N_KV * HEAD_DIM
QKV_COLS = ATT_W + 6 * KV_W
GATE_COLS = 3 * N_HEADS

LANES = 128
SUBLANES = 8
VMEM_LIMIT_BYTES = 56 * 1024 * 1024

TM_PROJ = 512
TQ = 256
ATT_TILES = 1
TK_SEL = 512
TS_MIX = 512
TS_FFN = 512
FFN_CHUNK = 256
FFN_DOWN_GROUP = 4
CONV_ROWS = 32

F32 = jnp.float32
BF16 = jnp.bfloat16


def _cparams(sem):
    return pltpu.CompilerParams(dimension_semantics=sem, vmem_limit_bytes=VMEM_LIMIT_BYTES)


def _const_spec(shape):
    nd = len(shape)
    return pl.BlockSpec(shape, lambda *_: (0,) * nd, pipeline_mode=pl.Buffered(1))


def _rms(x, g):
    ms = jnp.mean(x * x, axis=-1, keepdims=True)
    return (x * lax.rsqrt(ms + EPS)) * g


def _rope(v, cos, sin_signed):
    d = lax.broadcasted_iota(jnp.int32, v.shape, 1) & (HEAD_DIM - 1)
    half = ROT_DIM // 2
    sw = jnp.where(d < half, pltpu.roll(v, LANES - half, 1), pltpu.roll(v, half, 1))
    return v * cos + sw * sin_signed


def _perm_rows(ts, w):
    return SUBLANES * ((ts // SUBLANES) * (w // LANES) + 4)


def _perm_access(ts, w):
    n, wc = ts // SUBLANES, w // LANES
    pitch = n * wc + 4
    natural = lambda t0, c: pl.ds((t0 // n) * pitch + (t0 % n) * wc + c, SUBLANES, stride=wc)
    strided = lambda j, c: pl.ds(j * wc + c, SUBLANES, stride=pitch)
    return n, wc, natural, strided


def _reorder(val, pbuf, to_strided):
    ts, w = val.shape
    n, wc, natural, strided = _perm_access(ts, w)
    src, dst = (natural, strided) if to_strided else (strided, natural)
    for g in range(n):
        for c in range(wc):
            pbuf[src(g * SUBLANES, c) if to_strided else src(g, c), :] = (
                val[g * SUBLANES:(g + 1) * SUBLANES, c * LANES:(c + 1) * LANES])
    return jnp.concatenate(
        [jnp.concatenate([pbuf[dst(g, c) if to_strided else dst(g * SUBLANES, c), :] for c in range(wc)], axis=1)
         for g in range(n)], axis=0)


def _dependent_zero(x, reps):
    bits = pltpu.bitcast(x, jnp.uint32)
    zero = pltpu.bitcast(lax.shift_right_logical(lax.shift_right_logical(bits, jnp.uint32(16)), jnp.uint32(16)), F32)
    return jnp.concatenate([zero] * reps, axis=1)


def _shift_in_groups(prev_tail, tail):
    sub = lax.broadcasted_iota(jnp.int32, tail.shape, 0) & (SUBLANES - 1)
    merged = jnp.where(sub == SUBLANES - 1, prev_tail, tail)
    return jnp.concatenate(
        [pltpu.roll(merged[i:i + SUBLANES], 1, 0) for i in range(0, tail.shape[0], SUBLANES)], axis=0)


ROPE_TB = 2048
ROPE_ROWS = 4 * (ROT_DIM // 2)


def _rope_table_kernel(pos_ref, inv_ref, tab_ref):
    half = ROT_DIM // 2
    ang = pos_ref[...].astype(F32) * inv_ref[...]
    sin = jnp.sin(ang)
    tab_ref[0:half, :] = jnp.cos(ang)
    tab_ref[half:2 * half, :] = -sin
    tab_ref[2 * half:3 * half, :] = sin
    sub = lax.broadcasted_iota(jnp.int32, (half, ang.shape[1]), 0)
    tab_ref[3 * half:, :] = jnp.where(sub == 0, 1.0, 0.0)


def _rope_tables(pos_row, inv_col):
    n = pos_row.shape[1]
    return pl.pallas_call(
        _rope_table_kernel,
        grid=(n // ROPE_TB,),
        in_specs=[pl.BlockSpec((1, ROPE_TB), lambda i: (0, i)), _const_spec(inv_col.shape)],
        out_specs=pl.BlockSpec((ROPE_ROWS, ROPE_TB), lambda i: (0, i)),
        out_shape=jax.ShapeDtypeStruct((ROPE_ROWS, n), F32),
        compiler_params=_cparams(("parallel",)),
        name="rope_tables",
    )(pos_row, inv_col)


def _inproj_kernel(x_ref, g_ref, w_ref, cos_ref, sin_ref,
                   qpad_ref, kc_ref, vc_ref, ks0_ref, ks1_ref, vs0_ref, vs1_ref, kw_ref, vw0_ref, vw1_ref,
                   gate_ref, *, seq):
    tm = x_ref.shape[0]
    u = _rms(x_ref[...], g_ref[...])
    z = jnp.dot(u.astype(BF16), w_ref[...], preferred_element_type=F32)
    cos = cos_ref[...]
    sin = sin_ref[...]
    lane = lax.broadcasted_iota(jnp.int32, (tm, LANES), 1)
    low = lane < HEAD_DIM

    scale = HEAD_DIM ** -0.5 * LOG2_E
    for c in range(ATT_W // LANES):
        qc = _rope(z[:, c * LANES:(c + 1) * LANES], cos, sin) * scale
        qsw = pltpu.roll(qc, HEAD_DIM, 1)
        for e in range(2):
            hh = 2 * c + e
            grp = hh // HPG
            src = qc if e == grp else qsw
            keep = low if grp == 0 else jnp.logical_not(low)
            qpad_ref[:, hh * LANES:(hh + 1) * LANES] = jnp.where(keep, src, 0.0).astype(BF16)

    o = ATT_W
    kc_ref[...] = z[:, o:o + LANES]
    vc_ref[...] = z[:, o + LANES:o + 2 * LANES]
    ks = _rope(z[:, o + 2 * LANES:o + 3 * LANES], cos, sin)
    kw_ref[...] = _rope(z[:, o + 4 * LANES:o + 5 * LANES], cos, sin).astype(BF16)
    sig = jax.nn.sigmoid(z[:, o + 6 * LANES:o + 7 * LANES])
    gate_ref[...] = jnp.where(low, sig, pltpu.roll(sig, HEAD_DIM, 1))
    for v, (r0, r1) in ((z[:, o + 3 * LANES:o + 4 * LANES], (vs0_ref, vs1_ref)),
                        (z[:, o + 5 * LANES:o + 6 * LANES], (vw0_ref, vw1_ref))):
        r0[...] = jnp.where(low, v, 1.0).astype(BF16)
        r1[...] = jnp.where(low, 1.0, v).astype(BF16)

    s0 = (pl.program_id(0) * tm) % seq
    blk = (s0 + lax.broadcasted_iota(jnp.int32, (tm, LANES), 0)) // SEL_BLOCK
    hot0 = (lane - HEAD_DIM == blk).astype(F32)
    hot1 = (lane == blk).astype(F32)
    ks0_ref[...] = jnp.where(low, ks, hot0).astype(BF16)
    ks1_ref[...] = jnp.where(low, hot1, ks).astype(BF16)


def _inproj(x2, g, w_qkv, cos_t, sin_t, seq):
    n = x2.shape[0]
    tm = TM_PROJ
    ncols = w_qkv.shape[1]
    tok = lambda w: pl.BlockSpec((tm, w), lambda i: (i, 0))
    outs = [(N_HEADS * LANES, BF16), (LANES, F32), (LANES, F32)] + [(LANES, BF16)] * 7 + [(LANES, F32)]
    return pl.pallas_call(
        functools.partial(_inproj_kernel, seq=seq),
        grid=(n // tm,),
        in_specs=[tok(D_MODEL), _const_spec((1, D_MODEL)), _const_spec((D_MODEL, ncols)),
                  tok(LANES), tok(LANES)],
        out_specs=[tok(w) for w, _ in outs],
        out_shape=[jax.ShapeDtypeStruct((n, w), dt) for w, dt in outs],
        compiler_params=_cparams(("parallel",)),
        name="in_proj",
    )(x2, g, w_qkv, cos_t, sin_t)


def _compress_kernel(kx_ref, vx_ref, pek_ref, pev_ref, wka_ref, wkb_ref, wk2_ref,
                     wva_ref, wvb_ref, wv2_ref, cos_ref, sin_ref, kcmp_ref, vcmp_ref):
    def comp(x, pe_ref, wa_ref, wb_ref, w2_ref):
        n = x.shape[0]
        xa = (x + pe_ref[0:1, :]).astype(BF16)
        xb = (x + pe_ref[1:2, :]).astype(BF16)
        ya = jnp.dot(xa, wa_ref[...], preferred_element_type=F32)
        yb = jnp.dot(xb, wb_ref[...], preferred_element_type=F32)
        h = ya + pltpu.roll(yb, n - 1, 0)
        return jnp.dot(jax.nn.gelu(h).astype(BF16), w2_ref[...], preferred_element_type=F32)

    def stride_rows(ref):
        n = ref.shape[1] // CMP_STRIDE
        return jnp.concatenate([ref[0, pl.ds(r, n, stride=CMP_STRIDE), :] for r in range(CMP_STRIDE)], axis=1)

    kc = comp(stride_rows(kx_ref), pek_ref, wka_ref, wkb_ref, wk2_ref)
    kcmp_ref[0] = _rope(kc, cos_ref[...], sin_ref[...]).astype(BF16)
    vcmp_ref[0] = comp(stride_rows(vx_ref), pev_ref, wva_ref, wvb_ref, wv2_ref).astype(BF16)


def _compress(kx, vx, pek, pev, wka, wkb, wk2, wva, wvb, wv2, cos_t, sin_t, row0):
    b, seq, w = kx.shape
    n = seq // CMP_STRIDE
    per_b = lambda s: pl.BlockSpec((1,) + s, lambda i: (i, 0, 0))
    table = pl.BlockSpec((n, LANES), lambda i: (row0 // n + i, 0))
    return pl.pallas_call(
        _compress_kernel,
        grid=(b,),
        in_specs=[per_b((seq, w)), per_b((seq, w)), _const_spec(pek.shape), _const_spec(pev.shape),
                  _const_spec(wka.shape), _const_spec(wkb.shape), _const_spec(wk2.shape),
                  _const_spec(wva.shape), _const_spec(wvb.shape), _const_spec(wv2.shape),
                  table, table],
        out_specs=[per_b((n, LANES))] * 2,
        out_shape=[jax.ShapeDtypeStruct((b, n, LANES), BF16)] * 2,
        compiler_params=_cparams(("parallel",)),
        name="kv_compress",
    )(kx, vx, pek, pev, wka, wkb, wk2, wva, wvb, wv2, cos_t, sin_t)


def _nt_dot(a, b):
    return lax.dot_general(a, b, (((1,), (1,)), ((), ())), preferred_element_type=F32)


def _softmax_step(s, v, m, acc):
    m_new = jnp.maximum(m, jnp.max(s, axis=-1, keepdims=True))
    p = jnp.exp2(s - m_new)
    acc_new = jnp.exp2(m - m_new) * acc + jnp.dot(p.astype(BF16), v, preferred_element_type=F32)
    return m_new, acc_new


def _attn_kernel(*refs, seq, n_full):
    for c in range(ATT_TILES):
        _attn_tile(c, *refs, seq=seq, n_full=n_full)


def _attn_tile(c, q_ref, kcmp_ref, vcmp_ref, ks0_ref, ks1_ref, vs0_ref, vs1_ref, kw_ref, vw0_ref, vw1_ref,
               gate_ref, ovt_ref, out_ref, sbuf, *, seq, n_full):
    tq = TQ
    rs = slice(c * tq, (c + 1) * tq)
    t_end = (n_full + 1) * TK_SEL
    n_cmp = min(kcmp_ref.shape[1], -(-(t_end // CMP_STRIDE) // LANES) * LANES)
    n_sel = t_end // SEL_BLOCK
    top_n = min(SEL_TOPK, seq // SEL_BLOCK)
    rows = HPG * tq
    groups = range(N_KV)
    q0 = pl.multiple_of(n_full * TK_SEL + (pl.program_id(1) * ATT_TILES + c) * tq, tq)
    t_col = q0 + lax.broadcasted_iota(jnp.int32, (tq, 1), 0)
    t_row = q0 + lax.broadcasted_iota(jnp.int32, (1, tq), 1)
    low = lax.broadcasted_iota(jnp.int32, (tq, LANES), 1) < HEAD_DIM
    low4 = lax.broadcasted_iota(jnp.int32, (rows, LANES), 1) < HEAD_DIM
    in_grp4 = (low4, jnp.logical_not(low4))
    ks_refs, vs_refs, vw_refs = (ks0_ref, ks1_ref), (vs0_ref, vs1_ref), (vw0_ref, vw1_ref)

    def per_head(mask, s, fill):
        s3 = s.reshape(HPG, tq, s.shape[-1])
        return jnp.where(mask[None], s3, fill).reshape(s.shape)

    cmp_end = lax.broadcasted_iota(jnp.int32, (1, n_cmp), 1) * CMP_STRIDE + (CMP_BLOCK - 1)
    cmp_valid = cmp_end <= t_col
    kcmp = kcmp_ref[0, :n_cmp, :]
    vcmp = vcmp_ref[0, :n_cmp, :]
    gates = gate_ref[0, rs, :]
    qz = [jnp.concatenate(
        [q_ref[0, rs, (g * HPG + h) * LANES:(g * HPG + h + 1) * LANES] for h in range(HPG)], axis=0)
        for g in groups]

    init = (jnp.full((rows, 1), NEG, F32), jnp.zeros((rows, LANES), F32))

    sub = lax.broadcasted_iota(jnp.int32, (SUBLANES, tq), 0)
    cur = t_row // SEL_BLOCK
    n_blk_rows = n_sel // SUBLANES
    o_cmp, bias_t = [], []
    has_cmp = q0 + (lax.broadcasted_iota(jnp.int32, (rows, 1), 0) & (tq - 1)) >= CMP_BLOCK - 1
    for g in groups:
        s = per_head(cmp_valid, _nt_dot(qz[g], kcmp), NEG)
        m = jnp.max(s, axis=-1, keepdims=True)
        e = jnp.exp2(s - m)
        p = e * jnp.where(has_cmp, 1.0 / jnp.sum(e, axis=-1, keepdims=True), 0.0)
        o_cmp.append(jnp.dot(p.astype(BF16), vcmp, preferred_element_type=F32))
        p3 = p.reshape(HPG, tq, n_cmp)
        p_sum = p3[0]
        for h in range(1, HPG):
            p_sum = p_sum + p3[h]
        imp_t = lax.dot_general(ovt_ref[:, :n_cmp], p_sum, (((1,), (1,)), ((), ())),
                                preferred_element_type=F32, precision=lax.Precision.HIGHEST)

        score = []
        for v in range(n_blk_rows):
            jj = sub + v * SUBLANES
            forced = (jj == 0) | (jj == cur) | (jj == cur - 1)
            blk_valid = jj * SEL_BLOCK <= t_row
            score.append(jnp.where(blk_valid,
                                   jnp.where(forced, FORCED_SCORE, imp_t[v * SUBLANES:(v + 1) * SUBLANES]), -1.0))
        rank = [jnp.zeros((SUBLANES, tq), jnp.int32)] * n_blk_rows
        for i in range(n_sel):
            vi, ri = divmod(i, SUBLANES)
            si = score[vi][ri:ri + 1, :]
            for v in range(n_blk_rows):
                ge = jnp.where(si >= score[v], 1, 0)
                gt = jnp.where(si > score[v], 1, 0)
                rank[v] = rank[v] + (ge if v > vi else gt if v < vi else jnp.where(sub > ri, ge, gt))
        blocks = [jnp.where(r >= top_n, NEG, 0.0) for r in rank]
        blocks += [jnp.zeros((SUBLANES, tq), F32)] * (HEAD_DIM // SUBLANES - n_blk_rows)
        bias_t.append(jnp.concatenate(blocks, axis=0))

    bias = jnp.concatenate([bias_t[1], bias_t[0]], axis=0).T.astype(BF16)
    bias4 = jnp.concatenate([bias] * HPG, axis=0)
    qb = [jnp.where(in_grp4[g], qz[g], bias4) for g in groups]

    a0 = pl.multiple_of(jnp.maximum(q0 - WINDOW, 0), tq)
    kpos = a0 + lax.broadcasted_iota(jnp.int32, (1, WINDOW + tq), 1)
    wmask = (kpos <= t_col) & (t_col - kpos < WINDOW)
    o_win = []
    for g in groups:
        s = per_head(wmask, _nt_dot(qz[g], kw_ref[0, pl.ds(a0, WINDOW + tq), :]), NEG)
        o_win.append(_softmax_step(s, vw_refs[g][0, pl.ds(a0, WINDOW + tq), :], *init)[1])

    def put_scores(start, slot):
        for g in groups:
            sbuf[c, slot, g] = _nt_dot(qb[g], ks_refs[g][0, pl.ds(start, TK_SEL), :])

    def consume(start, s, state):
        out = []
        for g in groups:
            out.extend(_softmax_step(s[g], vs_refs[g][0, pl.ds(start, TK_SEL), :], state[2 * g], state[2 * g + 1]))
        return tuple(out)

    def sel_step(start, slot, state):
        put_scores(start + TK_SEL, 1 - slot)
        return consume(start, [sbuf[c, slot, g] for g in groups], state)

    put_scores(0, 0)
    state = init * N_KV
    for kt in range(n_full):
        state = sel_step(kt * TK_SEL, kt % 2, state)
    last = n_full * TK_SEL
    kpos = last + lax.broadcasted_iota(jnp.int32, (1, TK_SEL), 1)
    s_last = [per_head(kpos <= t_col, sbuf[c, n_full % 2, g], NEG) for g in groups]
    state = consume(last, s_last, state)
    o_sel = [state[2 * g + 1] for g in groups]

    for g in groups:
        heads = []
        for h in range(HPG):
            col = (HEAD_DIM if g == 0 else 0) + g * HPG + h
            r = slice(h * tq, (h + 1) * tq)
            w_sel = gates / o_sel[g][r]
            w_win = gates / o_win[g][r]
            heads.append(gates[:, col:col + 1] * o_cmp[g][r]
                         + w_sel[:, N_HEADS + col:N_HEADS + col + 1] * o_sel[g][r]
                         + w_win[:, 2 * N_HEADS + col:2 * N_HEADS + col + 1] * o_win[g][r])
        for c in range(HPG // 2):
            a, b = heads[2 * c], heads[2 * c + 1]
            if g == 0:
                pair = jnp.where(low, a, pltpu.roll(b, HEAD_DIM, 1))
            else:
                pair = jnp.where(low, pltpu.roll(a, HEAD_DIM, 1), b)
            oc = g * (HPG // 2) + c
            out_ref[0, rs, oc * LANES:(oc + 1) * LANES] = pair.astype(out_ref.dtype)


def _attention(qpad, kcmp, vcmp, seq_kv, gates, ov_t):
    b, seq, _ = qpad.shape
    n_cmp = kcmp.shape[1]
    per_b = lambda r: pl.BlockSpec((1, r, LANES), lambda i, j: (i, 0, 0))
    per_group = TK_SEL // (TQ * ATT_TILES)
    rows = TQ * ATT_TILES
    outs = []
    for n_full in range(seq // TK_SEL):
        q_tile = lambda w, n=n_full: pl.BlockSpec((1, rows, w), lambda i, j: (i, n * per_group + j, 0))
        outs.append(pl.pallas_call(
            functools.partial(_attn_kernel, seq=seq, n_full=n_full),
            grid=(b, per_group),
            in_specs=[q_tile(N_HEADS * LANES), per_b(n_cmp), per_b(n_cmp)] + [per_b(seq)] * len(seq_kv)
                     + [q_tile(LANES), _const_spec(ov_t.shape)],
            out_specs=pl.BlockSpec((1, rows, ATT_W), lambda i, j: (i, j, 0)),
            out_shape=jax.ShapeDtypeStruct((b, TK_SEL, ATT_W), BF16),
            scratch_shapes=[pltpu.VMEM((ATT_TILES, 2, N_KV, HPG * TQ, TK_SEL), F32)],
            compiler_params=_cparams(("parallel", "parallel")),
            name=f"nsa_attention_{n_full}",
        )(qpad, kcmp, vcmp, *seq_kv, gates, ov_t))
    return jnp.concatenate(outs, axis=1)


def _mix_kernel(x_ref, g_ref, wgm_ref, attn_ref, dww_ref, dwb_ref, lng_ref, lnb_ref,
                wa_ref, wb_ref, wo_ref, h_ref, cbuf, carry, ybuf, pbuf):
    ts = x_ref.shape[1]
    head = (CONV_WIDTH - 1) * SUBLANES

    @pl.when(pl.program_id(1) == 0)
    def _():
        carry[...] = jnp.zeros_like(carry)

    x = x_ref[0]
    u = _rms(x, g_ref[...]).astype(BF16)
    o = 2 * CONV_CH
    zg = jnp.dot(u, wgm_ref[:, :o], preferred_element_type=F32)
    glu = _reorder(zg[:, :CONV_CH] * jax.nn.sigmoid(zg[:, CONV_CH:]), pbuf, True)

    attn = attn_ref[0]
    blk = 2 * LANES
    side = [functools.partial(lambda n0: jnp.dot(u, wgm_ref[:, o + n0:o + n0 + blk], preferred_element_type=F32), n0)
            for n0 in range(0, 2 * D_MODEL, blk)]
    side += [functools.partial(lambda n0: jnp.dot(attn, wa_ref[:, n0:n0 + blk], preferred_element_type=F32), n0)
             for n0 in range(0, D_MODEL, blk)]
    side_out = []

    tail = glu[ts - head:, :]
    cbuf[0:head, :] = _shift_in_groups(carry[...], tail)
    cbuf[head:head + ts, :] = glu
    carry[...] = tail
    sub_groups = CONV_ROWS // SUBLANES
    pace = jnp.zeros((SUBLANES, CONV_CH), F32)
    for i, r0 in enumerate(range(0, ts, CONV_ROWS)):
        acc = [dwb_ref[...] + pace] + [dwb_ref[...]] * (sub_groups - 1)
        wts = [dww_ref[k * SUBLANES:(k + 1) * SUBLANES, :] for k in range(CONV_WIDTH)]
        for m in range(CONV_WIDTH - 1 + sub_groups):
            grp = cbuf[r0 + m * SUBLANES:r0 + (m + 1) * SUBLANES, :]
            for a in range(sub_groups):
                if 0 <= m - a < CONV_WIDTH:
                    acc[a] = acc[a] + wts[m - a] * grp
        ybuf[r0:r0 + CONV_ROWS, :] = jnp.concatenate(acc, axis=0)
        if i < len(side):
            side_out.append(side[i]())
            pace = _dependent_zero(side_out[-1][ts - SUBLANES:, :], CONV_CH // blk)
    side_out += [f() for f in side[len(side_out):]]
    zm = jnp.concatenate(side_out[:2 * D_MODEL // blk], axis=1)
    y_a = jnp.concatenate(side_out[2 * D_MODEL // blk:], axis=1)

    y = _reorder(ybuf[...], pbuf, False)
    mu = jnp.mean(y, axis=-1, keepdims=True)
    var = jnp.mean(jnp.square(y - mu), axis=-1, keepdims=True)
    yn = (y - mu) * lax.rsqrt(var + EPS) * lng_ref[...] + lnb_ref[...]
    c = yn * jax.nn.sigmoid(yn)
    y_b = jnp.dot(c.astype(BF16), wb_ref[...], preferred_element_type=F32)
    mix = jax.nn.sigmoid(zm[:, :D_MODEL]) * y_a + jax.nn.sigmoid(zm[:, D_MODEL:]) * y_b
    h_ref[0] = x + jnp.dot(mix.astype(BF16), wo_ref[...], preferred_element_type=F32)


def _mix(x, g, wgm, attn, dww, dwb, lng, lnb, wa, wb, wo):
    b, seq, _ = x.shape
    ts = TS_MIX
    tile = lambda w: pl.BlockSpec((1, ts, w), lambda i, j: (i, j, 0))
    consts = [g, wgm]
    consts2 = [dww, dwb, lng, lnb, wa, wb, wo]
    return pl.pallas_call(
        _mix_kernel,
        grid=(b, seq // ts),
        in_specs=[tile(D_MODEL)] + [_const_spec(a.shape) for a in consts]
                 + [tile(ATT_W)] + [_const_spec(a.shape) for a in consts2],
        out_specs=tile(D_MODEL),
        out_shape=jax.ShapeDtypeStruct((b, seq, D_MODEL), F32),
        scratch_shapes=[pltpu.VMEM(((CONV_WIDTH - 1) * SUBLANES + ts, CONV_CH), F32),
                        pltpu.VMEM(((CONV_WIDTH - 1) * SUBLANES, CONV_CH), F32),
                        pltpu.VMEM((ts, CONV_CH), F32),
                        pltpu.VMEM((_perm_rows(ts, CONV_CH), LANES), F32)],
        compiler_params=_cparams(("arbitrary", "arbitrary")),
        name="conv_merge",
    )(x, g, wgm, attn, dww, dwb, lng, lnb, wa, wb, wo)


def _ffn_kernel(h_ref, gf_ref, wup_ref, fw_ref, fb_ref, wdn_ref, p_ref, gp_ref, wpg_ref, wpp_ref,
                gfin_ref, out_ref, ubuf, carry, actbuf, hbuf, pbuf, *, final_norm):
    ts = h_ref.shape[1]
    fc = FFN_CHUNK
    n_chunks = FFN_DIM // fc
    head = (FFN_CONV_WIDTH - 1) * SUBLANES

    @pl.when(pl.program_id(1) == 0)
    def _():
        carry[...] = jnp.zeros_like(carry)

    h = _reorder(h_ref[0], hbuf, True)
    u = _rms(h, gf_ref[...]).astype(BF16)

    def conv_half(c, col0, slot):
        cols = slice(col0 + c * fc, col0 + (c + 1) * fc)
        up = jnp.dot(u, wup_ref[:, cols], preferred_element_type=F32)
        tail = up[ts - head:, :]
        buf = ubuf.at[slot]
        buf[0:head, :] = _shift_in_groups(carry[:, cols], tail)
        buf[head:head + ts, :] = up
        carry[:, cols] = tail
        conv = fb_ref[:, cols]
        for k in range(FFN_CONV_WIDTH):
            conv = conv + fw_ref[k:k + 1, cols] * buf[k * SUBLANES:k * SUBLANES + ts, :]
        return conv

    h2 = h
    for c0 in range(0, n_chunks, FFN_DOWN_GROUP):
        c1 = min(c0 + FFN_DOWN_GROUP, n_chunks)
        for c in range(c0, c1):
            a = conv_half(c, 0, (2 * c) % 4)
            gt = conv_half(c, FFN_DIM, (2 * c + 1) % 4)
            actbuf[:, c * fc:(c + 1) * fc] = (gt * jax.nn.sigmoid(gt) * a).astype(BF16)
        h2 = h2 + jnp.dot(actbuf[:, c0 * fc:c1 * fc], wdn_ref[c0 * fc:c1 * fc, :], preferred_element_type=F32)

    u3 = _rms(h2, gp_ref[...]).astype(BF16)
    gate = jax.nn.sigmoid(jnp.dot(u3, wpg_ref[...], preferred_element_type=F32))
    proj = jnp.dot(_reorder(p_ref[0], pbuf, True).astype(BF16), wpp_ref[...], preferred_element_type=F32)
    h3 = h2 + gate * proj
    out_ref[0] = _reorder(_rms(h3, gfin_ref[...]) if final_norm else h3, hbuf, False)


def _ffn(h, gf, wup, fw, fb, wdn, p, gp, wpg, wpp, gfin, final_norm):
    b, seq, _ = h.shape
    ts = TS_FFN
    tile = lambda w: pl.BlockSpec((1, ts, w), lambda i, j: (i, j, 0))
    single = lambda a: _const_spec(a.shape)
    return pl.pallas_call(
        functools.partial(_ffn_kernel, final_norm=final_norm),
        grid=(b, seq // ts),
        in_specs=[tile(D_MODEL), single(gf), single(wup), single(fw), single(fb), single(wdn),
                  tile(PLE_DIM), single(gp), single(wpg), single(wpp), single(gfin)],
        out_specs=tile(D_MODEL),
        out_shape=jax.ShapeDtypeStruct((b, seq, D_MODEL), F32),
        scratch_shapes=[pltpu.VMEM((4, (FFN_CONV_WIDTH - 1) * SUBLANES + ts, FFN_CHUNK), F32),
                        pltpu.VMEM(((FFN_CONV_WIDTH - 1) * SUBLANES, 2 * FFN_DIM), F32),
                        pltpu.VMEM((ts, FFN_DIM), BF16),
                        pltpu.VMEM((_perm_rows(ts, D_MODEL), LANES), F32),
                        pltpu.VMEM((_perm_rows(ts, PLE_DIM), LANES), F32)],
        compiler_params=_cparams(("arbitrary", "arbitrary")),
        name="conv_ffn",
    )(h, gf, wup, fw, fb, wdn, p, gp, wpg, wpp, gfin)


def _expand_cmp_weights(w1, w2):
    half = CMP_BLOCK // 2
    w1r = w1.reshape(2, half, HEAD_DIM, CMP_HIDDEN).astype(BF16)
    w2b = w2.astype(BF16)
    z1, z2 = jnp.zeros_like(w1r), jnp.zeros_like(w2b)
    wab = jnp.stack([jnp.concatenate([w1r, z1], axis=-1), jnp.concatenate([z1, w1r], axis=-1)], axis=2)
    wab = wab.reshape(2, half * KV_W, N_KV * CMP_HIDDEN)
    w2e = jnp.concatenate([jnp.concatenate([w2b, z2], axis=-1), jnp.concatenate([z2, w2b], axis=-1)], axis=0)
    return wab[0], wab[1], w2e


def _expand_pe(pe):
    half = CMP_BLOCK // 2
    per = jnp.broadcast_to(pe.reshape(2, half, 1, HEAD_DIM), (2, half, N_KV, HEAD_DIM))
    return per.reshape(2, half * KV_W)


def kernel(x, p, positions, norm_mix_g, w_in, pe_k, pe_v, cmp_k_w1, cmp_k_w2, cmp_v_w1, cmp_v_w2, conv_dw_w, conv_dw_b, conv_ln_g, conv_ln_b, w_a, w_b, w_o, norm_ffn_g, w_up, ffn_dw_w, ffn_dw_b, w_down, norm_ple_g, w_ple_gate, w_ple_proj, norm_final_g):
    b, seq, _ = x.shape
    depth = w_in.shape[0]
    n_tok = b * seq
    n_cmp = seq // CMP_STRIDE
    n_sel = seq // SEL_BLOCK
    assert seq % TS_MIX == 0 and seq % TK_SEL == 0 and seq >= WINDOW and n_sel <= HEAD_DIM
    assert FFN_DIM % FFN_CHUNK == 0 and seq % TS_FFN == 0 and N_KV == 2

    half = ROT_DIM // 2
    inv_col = (ROPE_THETA ** (-jnp.arange(0, ROT_DIM, 2, dtype=F32) / ROT_DIM)).reshape(half, 1)
    cmp_start = np.arange(n_cmp) * CMP_STRIDE
    sel_start = np.arange(n_sel) * SEL_BLOCK
    overlap = np.maximum(np.minimum(cmp_start[:, None] + CMP_BLOCK, sel_start[None, :] + SEL_BLOCK)
                         - np.maximum(cmp_start[:, None], sel_start[None, :]), 0).astype(np.float32) / CMP_STRIDE
    ov_t = jnp.asarray(np.pad(overlap.T, ((0, HEAD_DIM - n_sel), (0, 0))))

    cmp_end = np.minimum(np.arange(n_cmp) * CMP_STRIDE + CMP_BLOCK - 1, seq - 1)
    pos_all = jnp.concatenate([positions.reshape(-1), positions[:, cmp_end].reshape(-1)])
    pos_all = jnp.pad(pos_all, (0, (-pos_all.shape[0]) % ROPE_TB)).reshape(1, -1)
    tab = _rope_tables(pos_all, inv_col)
    d = np.arange(LANES) % HEAD_DIM
    cos_t = tab[np.where(d < ROT_DIM, d % half, 3 * half)].T
    sin_t = tab[np.where(d < half, half + d, np.where(d < ROT_DIM, half + d, 3 * half + 1))].T
    assert n_tok % n_cmp == 0

    h = x
    row = lambda v: v.reshape(1, -1)
    for i in range(depth):
        w_qkv = jnp.concatenate(
            [w_in[i][:, :QKV_COLS + GATE_COLS],
             jnp.zeros((D_MODEL, LANES - GATE_COLS), w_in.dtype)], axis=1).astype(BF16)
        w_gm = w_in[i][:, QKV_COLS + GATE_COLS:].astype(BF16)

        qpad, kc, vc, *seq_kv, gates = _inproj(
            h.reshape(n_tok, D_MODEL), row(norm_mix_g[i]), w_qkv, cos_t, sin_t, seq)

        wka, wkb, wk2 = _expand_cmp_weights(cmp_k_w1[i], cmp_k_w2[i])
        wva, wvb, wv2 = _expand_cmp_weights(cmp_v_w1[i], cmp_v_w2[i])
        kcmp, vcmp = _compress(kc.reshape(b, seq, KV_W), vc.reshape(b, seq, KV_W),
                               _expand_pe(pe_k[i]), _expand_pe(pe_v[i]),
                               wka, wkb, wk2, wva, wvb, wv2, cos_t, sin_t, n_tok)

        s3 = lambda a: a.reshape(b, seq, a.shape[-1])
        attn = _attention(s3(qpad), kcmp, vcmp, [s3(a) for a in seq_kv], s3(gates), ov_t)

        h = _mix(h, row(norm_mix_g[i]), w_gm, attn,
                 jnp.repeat(conv_dw_w[i].reshape(CONV_WIDTH, CONV_CH), SUBLANES, axis=0),
                 jnp.broadcast_to(row(conv_dw_b[i]), (SUBLANES, CONV_CH)),
                 row(conv_ln_g[i]), row(conv_ln_b[i]),
                 w_a[i].astype(BF16), w_b[i].astype(BF16), w_o[i].astype(BF16))

        h = _ffn(h, row(norm_ffn_g[i]), w_up[i].astype(BF16),
                 ffn_dw_w[i].reshape(FFN_CONV_WIDTH, 2 * FFN_DIM), row(ffn_dw_b[i]),
                 w_down[i].astype(BF16), p[i], row(norm_ple_g[i]),
                 w_ple_gate[i].astype(BF16), w_ple_proj[i].astype(BF16), row(norm_final_g),
                 final_norm=(i == depth - 1))
    return h
```

```python
import functools

import numpy as np
import jax
import jax.numpy as jnp
from jax import lax
from jax.experimental import pallas as pl
from jax.experimental.pallas import tpu as pltpu

D_MODEL = 1024
PLE_DIM = 256
N_HEADS = 8
N_KV = 2
HPG = N_HEADS // N_KV
HEAD_DIM = 64
ROT_DIM = HEAD_DIM // 4
ROPE_THETA = 500000.0
CMP_BLOCK = 32
CMP_STRIDE = 16
CMP_HIDDEN = 128
SEL_BLOCK = 64
SEL_TOPK = 16
WINDOW = 512
CONV_CH = 512
CONV_WIDTH = 31
FFN_DIM = 2816
FFN_CONV_WIDTH = 3
EPS = 1e-6
NEG = -1e30
FORCED_SCORE = 1e6
LOG2_E = 1.4426950408889634

ATT_W = N_HEADS * HEAD_DIM
KV_W = N_KV * HEAD_DIM
QKV_COLS = ATT_W + 6 * KV_W
GATE_COLS = 3 * N_HEADS

LANES = 128
SUBLANES = 8
VMEM_LIMIT_BYTES = 56 * 1024 * 1024

TM_PROJ = 512
TQ = 256
ATT_TILES = 1
TK_SEL = 512
TS_MIX = 512
TS_FFN = 512
FFN_CHUNK = 256
FFN_DOWN_GROUP = 4
CONV_ROWS = 32

F32 = jnp.float32
BF16 = jnp.bfloat16


def _cparams(sem):
    return pltpu.CompilerParams(dimension_semantics=sem, vmem_limit_bytes=VMEM_LIMIT_BYTES)


def _const_spec(shape):
    nd = len(shape)
    return pl.BlockSpec(shape, lambda *_: (0,) * nd, pipeline_mode=pl.Buffered(1))


def _rms(x, g):
    ms = jnp.mean(x * x, axis=-1, keepdims=True)
    return (x * lax.rsqrt(ms + EPS)) * g


def _rope(v, cos, sin_signed):
    d = lax.broadcasted_iota(jnp.int32, v.shape, 1) & (HEAD_DIM - 1)
    half = ROT_DIM // 2
    sw = jnp.where(d < half, pltpu.roll(v, LANES - half, 1), pltpu.roll(v, half, 1))
    return v * cos + sw * sin_signed


def _perm_rows(ts, w):
    return SUBLANES * ((ts // SUBLANES) * (w // LANES) + 4)


def _perm_access(ts, w):
    n, wc = ts // SUBLANES, w // LANES
    pitch = n * wc + 4
    natural = lambda t0, c: pl.ds((t0 // n) * pitch + (t0 % n) * wc + c, SUBLANES, stride=wc)
    strided = lambda j, c: pl.ds(j * wc + c, SUBLANES, stride=pitch)
    return n, wc, natural, strided


def _reorder(val, pbuf, to_strided):
    ts, w = val.shape
    n, wc, natural, strided = _perm_access(ts, w)
    src, dst = (natural, strided) if to_strided else (strided, natural)
    for g in range(n):
        for c in range(wc):
            pbuf[src(g * SUBLANES, c) if to_strided else src(g, c), :] = (
                val[g * SUBLANES:(g + 1) * SUBLANES, c * LANES:(c + 1) * LANES])
    return jnp.concatenate(
        [jnp.concatenate([pbuf[dst(g, c) if to_strided else dst(g * SUBLANES, c), :] for c in range(wc)], axis=1)
         for g in range(n)], axis=0)


def _dependent_zero(x, reps):
    bits = pltpu.bitcast(x, jnp.uint32)
    zero = pltpu.bitcast(lax.shift_right_logical(lax.shift_right_logical(bits, jnp.uint32(16)), jnp.uint32(16)), F32)
    return jnp.concatenate([zero] * reps, axis=1)


def _shift_in_groups(prev_tail, tail):
    sub = lax.broadcasted_iota(jnp.int32, tail.shape, 0) & (SUBLANES - 1)
    merged = jnp.where(sub == SUBLANES - 1, prev_tail, tail)
    return jnp.concatenate(
        [pltpu.roll(merged[i:i + SUBLANES], 1, 0) for i in range(0, tail.shape[0], SUBLANES)], axis=0)


ROPE_TB = 2048
ROPE_ROWS = 4 * (ROT_DIM // 2)


def _rope_table_kernel(pos_ref, inv_ref, tab_ref):
    half = ROT_DIM // 2
    ang = pos_ref[...].astype(F32) * inv_ref[...]
    sin = jnp.sin(ang)
    tab_ref[0:half, :] = jnp.cos(ang)
    tab_ref[half:2 * half, :] = -sin
    tab_ref[2 * half:3 * half, :] = sin
    sub = lax.broadcasted_iota(jnp.int32, (half, ang.shape[1]), 0)
    tab_ref[3 * half:, :] = jnp.where(sub == 0, 1.0, 0.0)


def _rope_tables(pos_row, inv_col):
    n = pos_row.shape[1]
    return pl.pallas_call(
        _rope_table_kernel,
        grid=(n // ROPE_TB,),
        in_specs=[pl.BlockSpec((1, ROPE_TB), lambda i: (0, i)), _const_spec(inv_col.shape)],
        out_specs=pl.BlockSpec((ROPE_ROWS, ROPE_TB), lambda i: (0, i)),
        out_shape=jax.ShapeDtypeStruct((ROPE_ROWS, n), F32),
        compiler_params=_cparams(("parallel",)),
        name="rope_tables",
    )(pos_row, inv_col)


def _inproj_kernel(x_ref, g_ref, w_ref, cos_ref, sin_ref,
                   qpad_ref, kc_ref, vc_ref, ks0_ref, ks1_ref, vs0_ref, vs1_ref, kw_ref, vw0_ref, vw1_ref,
                   gate_ref, *, seq):
    tm = x_ref.shape[0]
    u = _rms(x_ref[...], g_ref[...])
    z = jnp.dot(u.astype(BF16), w_ref[...], preferred_element_type=F32)
    cos = cos_ref[...]
    sin = sin_ref[...]
    lane = lax.broadcasted_iota(jnp.int32, (tm, LANES), 1)
    low = lane < HEAD_DIM

    scale = HEAD_DIM ** -0.5 * LOG2_E
    for c in range(ATT_W // LANES):
        qc = _rope(z[:, c * LANES:(c + 1) * LANES], cos, sin) * scale
        qsw = pltpu.roll(qc, HEAD_DIM, 1)
        for e in range(2):
            hh = 2 * c + e
            grp = hh // HPG
            src = qc if e == grp else qsw
            keep = low if grp == 0 else jnp.logical_not(low)
            qpad_ref[:, hh * LANES:(hh + 1) * LANES] = jnp.where(keep, src, 0.0).astype(BF16)

    o = ATT_W
    kc_ref[...] = z[:, o:o + LANES]
    vc_ref[...] = z[:, o + LANES:o + 2 * LANES]
    ks = _rope(z[:, o + 2 * LANES:o + 3 * LANES], cos, sin)
    kw_ref[...] = _rope(z[:, o + 4 * LANES:o + 5 * LANES], cos, sin).astype(BF16)
    sig = jax.nn.sigmoid(z[:, o + 6 * LANES:o + 7 * LANES])
    gate_ref[...] = jnp.where(low, sig, pltpu.roll(sig, HEAD_DIM, 1))
    for v, (r0, r1) in ((z[:, o + 3 * LANES:o + 4 * LANES], (vs0_ref, vs1_ref)),
                        (z[:, o + 5 * LANES:o + 6 * LANES], (vw0_ref, vw1_ref))):
        r0[...] = jnp.where(low, v, 1.0).astype(BF16)
        r1[...] = jnp.where(low, 1.0, v).astype(BF16)

    s0 = (pl.program_id(0) * tm) % seq
    blk = (s0 + lax.broadcasted_iota(jnp.int32, (tm, LANES), 0)) // SEL_BLOCK
    hot0 = (lane - HEAD_DIM == blk).astype(F32)
    hot1 = (lane == blk).astype(F32)
    ks0_ref[...] = jnp.where(low, ks, hot0).astype(BF16)
    ks1_ref[...] = jnp.where(low, hot1, ks).astype(BF16)


def _inproj(x2, g, w_all, cos_t, sin_t, seq):
    n = x2.shape[0]
    tm = TM_PROJ
    ncols = QKV_COLS + LANES
    tok = lambda w: pl.BlockSpec((tm, w), lambda i: (i, 0))
    outs = [(N_HEADS * LANES, BF16), (LANES, F32), (LANES, F32)] + [(LANES, BF16)] * 7 + [(LANES, F32)]
    return pl.pallas_call(
        functools.partial(_inproj_kernel, seq=seq),
        grid=(n // tm,),
        in_specs=[tok(D_MODEL), _const_spec((1, D_MODEL)), _const_spec((D_MODEL, ncols)),
                  tok(LANES), tok(LANES)],
        out_specs=[tok(w) for w, _ in outs],
        out_shape=[jax.ShapeDtypeStruct((n, w), dt) for w, dt in outs],
        compiler_params=_cparams(("parallel",)),
        name="in_proj",
    )(x2, g, w_all, cos_t, sin_t)


def _compress_kernel(kx_ref, vx_ref, pek_ref, pev_ref, wka_ref, wkb_ref, wk2_ref,
                     wva_ref, wvb_ref, wv2_ref, cos_ref, sin_ref, kcmp_ref, vcmp_ref):
    def comp(x, pe_ref, wa_ref, wb_ref, w2_ref):
        n = x.shape[0]
        xa = (x + pe_ref[0:1, :]).astype(BF16)
        xb = (x + pe_ref[1:2, :]).astype(BF16)
        ya = jnp.dot(xa, wa_ref[...], preferred_element_type=F32)
        yb = jnp.dot(xb, wb_ref[...], preferred_element_type=F32)
        h = ya + pltpu.roll(yb, n - 1, 0)
        return jnp.dot(jax.nn.gelu(h).astype(BF16), w2_ref[...], preferred_element_type=F32)

    def stride_rows(ref):
        n = ref.shape[1] // CMP_STRIDE
        return jnp.concatenate([ref[0, pl.ds(r, n, stride=CMP_STRIDE), :] for r in range(CMP_STRIDE)], axis=1)

    kc = comp(stride_rows(kx_ref), pek_ref, wka_ref, wkb_ref, wk2_ref)
    kcmp_ref[0] = _rope(kc, cos_ref[...], sin_ref[...]).astype(BF16)
    vcmp_ref[0] = comp(stride_rows(vx_ref), pev_ref, wva_ref, wvb_ref, wv2_ref).astype(BF16)


def _compress(kx, vx, pek, pev, wka, wkb, wk2, wva, wvb, wv2, cos_t, sin_t, row0):
    b, seq, w = kx.shape
    n = seq // CMP_STRIDE
    per_b = lambda s: pl.BlockSpec((1,) + s, lambda i: (i, 0, 0))
    table = pl.BlockSpec((n, LANES), lambda i: (row0 // n + i, 0))
    return pl.pallas_call(
        _compress_kernel,
        grid=(b,),
        in_specs=[per_b((seq, w)), per_b((seq, w)), _const_spec(pek.shape), _const_spec(pev.shape),
                  _const_spec(wka.shape), _const_spec(wkb.shape), _const_spec(wk2.shape),
                  _const_spec(wva.shape), _const_spec(wvb.shape), _const_spec(wv2.shape),
                  table, table],
        out_specs=[per_b((n, LANES))] * 2,
        out_shape=[jax.ShapeDtypeStruct((b, n, LANES), BF16)] * 2,
        compiler_params=_cparams(("parallel",)),
        name="kv_compress",
    )(kx, vx, pek, pev, wka, wkb, wk2, wva, wvb, wv2, cos_t, sin_t)


def _nt_dot(a, b):
    return lax.dot_general(a, b, (((1,), (1,)), ((), ())), preferred_element_type=F32)


def _softmax_step(s, v, m, acc):
    m_new = jnp.maximum(m, jnp.max(s, axis=-1, keepdims=True))
    p = jnp.exp2(s - m_new)
    acc_new = jnp.exp2(m - m_new) * acc + jnp.dot(p.astype(BF16), v, preferred_element_type=F32)
    return m_new, acc_new


def _attn_kernel(*refs, seq, n_full):
    for c in range(ATT_TILES):
        _attn_tile(c, *refs, seq=seq, n_full=n_full)


def _attn_tile(c, q_ref, kcmp_ref, vcmp_ref, ks0_ref, ks1_ref, vs0_ref, vs1_ref, kw_ref, vw0_ref, vw1_ref,
               gate_ref, ovt_ref, out_ref, sbuf, *, seq, n_full):
    tq = TQ
    rs = slice(c * tq, (c + 1) * tq)
    t_end = (n_full + 1) * TK_SEL
    n_cmp = min(kcmp_ref.shape[1], -(-(t_end // CMP_STRIDE) // LANES) * LANES)
    n_sel = t_end // SEL_BLOCK
    top_n = min(SEL_TOPK, seq // SEL_BLOCK)
    rows = HPG * tq
    groups = range(N_KV)
    q0 = pl.multiple_of(n_full * TK_SEL + (pl.program_id(1) * ATT_TILES + c) * tq, tq)
    t_col = q0 + lax.broadcasted_iota(jnp.int32, (tq, 1), 0)
    t_row = q0 + lax.broadcasted_iota(jnp.int32, (1, tq), 1)
    low = lax.broadcasted_iota(jnp.int32, (tq, LANES), 1) < HEAD_DIM
    low4 = lax.broadcasted_iota(jnp.int32, (rows, LANES), 1) < HEAD_DIM
    in_grp4 = (low4, jnp.logical_not(low4))
    ks_refs, vs_refs, vw_refs = (ks0_ref, ks1_ref), (vs0_ref, vs1_ref), (vw0_ref, vw1_ref)

    def per_head(mask, s, fill):
        s3 = s.reshape(HPG, tq, s.shape[-1])
        return jnp.where(mask[None], s3, fill).reshape(s.shape)

    cmp_end = lax.broadcasted_iota(jnp.int32, (1, n_cmp), 1) * CMP_STRIDE + (CMP_BLOCK - 1)
    cmp_valid = cmp_end <= t_col
    kcmp = kcmp_ref[0, :n_cmp, :]
    vcmp = vcmp_ref[0, :n_cmp, :]
    gates = gate_ref[0, rs, :]
    qz = [jnp.concatenate(
        [q_ref[0, rs, (g * HPG + h) * LANES:(g * HPG + h + 1) * LANES] for h in range(HPG)], axis=0)
        for g in groups]

    init = (jnp.full((rows, 1), NEG, F32), jnp.zeros((rows, LANES), F32))

    sub = lax.broadcasted_iota(jnp.int32, (SUBLANES, tq), 0)
    cur = t_row // SEL_BLOCK
    n_blk_rows = n_sel // SUBLANES
    o_cmp, bias_t = [], []
    has_cmp = q0 + (lax.broadcasted_iota(jnp.int32, (rows, 1), 0) & (tq - 1)) >= CMP_BLOCK - 1
    for g in groups:
        s = per_head(cmp_valid, _nt_dot(qz[g], kcmp), NEG)
        m = jnp.max(s, axis=-1, keepdims=True)
        e = jnp.exp2(s - m)
        p = e * jnp.where(has_cmp, 1.0 / jnp.sum(e, axis=-1, keepdims=True), 0.0)
        o_cmp.append(jnp.dot(p.astype(BF16), vcmp, preferred_element_type=F32))
        p3 = p.reshape(HPG, tq, n_cmp)
        p_sum = p3[0]
        for h in range(1, HPG):
            p_sum = p_sum + p3[h]
        imp_t = lax.dot_general(ovt_ref[:, :n_cmp], p_sum, (((1,), (1,)), ((), ())),
                                preferred_element_type=F32, precision=lax.Precision.HIGHEST)

        score = []
        for v in range(n_blk_rows):
            jj = sub + v * SUBLANES
            forced = (jj == 0) | (jj == cur) | (jj == cur - 1)
            blk_valid = jj * SEL_BLOCK <= t_row
            score.append(jnp.where(blk_valid,
                                   jnp.where(forced, FORCED_SCORE, imp_t[v * SUBLANES:(v + 1) * SUBLANES]), -1.0))
        rank = [jnp.zeros((SUBLANES, tq), jnp.int32)] * n_blk_rows
        for i in range(n_sel):
            vi, ri = divmod(i, SUBLANES)
            si = score[vi][ri:ri + 1, :]
            for v in range(n_blk_rows):
                ge = jnp.where(si >= score[v], 1, 0)
                gt = jnp.where(si > score[v], 1, 0)
                rank[v] = rank[v] + (ge if v > vi else gt if v < vi else jnp.where(sub > ri, ge, gt))
        blocks = [jnp.where(r >= top_n, NEG, 0.0) for r in rank]
        blocks += [jnp.zeros((SUBLANES, tq), F32)] * (HEAD_DIM // SUBLANES - n_blk_rows)
        bias_t.append(jnp.concatenate(blocks, axis=0))

    bias = jnp.concatenate([bias_t[1], bias_t[0]], axis=0).T.astype(BF16)
    bias4 = jnp.concatenate([bias] * HPG, axis=0)
    qb = [jnp.where(in_grp4[g], qz[g], bias4) for g in groups]

    a0 = pl.multiple_of(jnp.maximum(q0 - WINDOW, 0), tq)
    kpos = a0 + lax.broadcasted_iota(jnp.int32, (1, WINDOW + tq), 1)
    wmask = (kpos <= t_col) & (t_col - kpos < WINDOW)
    o_win = []
    for g in groups:
        s = per_head(wmask, _nt_dot(qz[g], kw_ref[0, pl.ds(a0, WINDOW + tq), :]), NEG)
        o_win.append(_softmax_step(s, vw_refs[g][0, pl.ds(a0, WINDOW + tq), :], *init)[1])

    def put_scores(start, slot):
        for g in groups:
            sbuf[c, slot, g] = _nt_dot(qb[g], ks_refs[g][0, pl.ds(start, TK_SEL), :])

    def consume(start, s, state):
        out = []
        for g in groups:
            out.extend(_softmax_step(s[g], vs_refs[g][0, pl.ds(start, TK_SEL), :], state[2 * g], state[2 * g + 1]))
        return tuple(out)

    def sel_step(start, slot, state):
        put_scores(start + TK_SEL, 1 - slot)
        return consume(start, [sbuf[c, slot, g] for g in groups], state)

    put_scores(0, 0)
    state = init * N_KV
    for kt in range(n_full):
        state = sel_step(kt * TK_SEL, kt % 2, state)
    last = n_full * TK_SEL
    kpos = last + lax.broadcasted_iota(jnp.int32, (1, TK_SEL), 1)
    s_last = [per_head(kpos <= t_col, sbuf[c, n_full % 2, g], NEG) for g in groups]
    state = consume(last, s_last, state)
    o_sel = [state[2 * g + 1] for g in groups]

    for g in groups:
        heads = []
        for h in range(HPG):
            col = (HEAD_DIM if g == 0 else 0) + g * HPG + h
            r = slice(h * tq, (h + 1) * tq)
            w_sel = gates / o_sel[g][r]
            w_win = gates / o_win[g][r]
            heads.append(gates[:, col:col + 1] * o_cmp[g][r]
                         + w_sel[:, N_HEADS + col:N_HEADS + col + 1] * o_sel[g][r]
                         + w_win[:, 2 * N_HEADS + col:2 * N_HEADS + col + 1] * o_win[g][r])
        for c in range(HPG // 2):
            a, b = heads[2 * c], heads[2 * c + 1]
            if g == 0:
                pair = jnp.where(low, a, pltpu.roll(b, HEAD_DIM, 1))
            else:
                pair = jnp.where(low, pltpu.roll(a, HEAD_DIM, 1), b)
            oc = g * (HPG // 2) + c
            out_ref[0, rs, oc * LANES:(oc + 1) * LANES] = pair.astype(out_ref.dtype)


def _attention(qpad, kcmp, vcmp, seq_kv, gates, ov_t):
    b, seq, _ = qpad.shape
    n_cmp = kcmp.shape[1]
    per_b = lambda r: pl.BlockSpec((1, r, LANES), lambda i, j: (i, 0, 0))
    per_group = TK_SEL // (TQ * ATT_TILES)
    rows = TQ * ATT_TILES
    outs = []
    for n_full in range(seq // TK_SEL):
        q_tile = lambda w, n=n_full: pl.BlockSpec((1, rows, w), lambda i, j: (i, n * per_group + j, 0))
        outs.append(pl.pallas_call(
            functools.partial(_attn_kernel, seq=seq, n_full=n_full),
            grid=(b, per_group),
            in_specs=[q_tile(N_HEADS * LANES), per_b(n_cmp), per_b(n_cmp)] + [per_b(seq)] * len(seq_kv)
                     + [q_tile(LANES), _const_spec(ov_t.shape)],
            out_specs=pl.BlockSpec((1, rows, ATT_W), lambda i, j: (i, j, 0)),
            out_shape=jax.ShapeDtypeStruct((b, TK_SEL, ATT_W), BF16),
            scratch_shapes=[pltpu.VMEM((ATT_TILES, 2, N_KV, HPG * TQ, TK_SEL), F32)],
            compiler_params=_cparams(("parallel", "parallel")),
            name=f"nsa_attention_{n_full}",
        )(qpad, kcmp, vcmp, *seq_kv, gates, ov_t))
    return jnp.concatenate(outs, axis=1)


def _mix_kernel(x_ref, g_ref, wgm_ref, attn_ref, dww_ref, dwb_ref, lng_ref, lnb_ref,
                wa_ref, wb_ref, wo_ref, h_ref, cbuf, carry, ybuf, pbuf):
    ts = x_ref.shape[1]
    head = (CONV_WIDTH - 1) * SUBLANES

    @pl.when(pl.program_id(1) == 0)
    def _():
        carry[...] = jnp.zeros_like(carry)

    x = x_ref[0]
    u = _rms(x, g_ref[...]).astype(BF16)
    o = 2 * CONV_CH
    zg = jnp.dot(u, wgm_ref[:, :o], preferred_element_type=F32)
    glu = _reorder(zg[:, :CONV_CH] * jax.nn.sigmoid(zg[:, CONV_CH:]), pbuf, True)

    attn = attn_ref[0]
    blk = 2 * LANES
    side = [functools.partial(lambda n0: jnp.dot(u, wgm_ref[:, o + n0:o + n0 + blk], preferred_element_type=F32), n0)
            for n0 in range(0, 2 * D_MODEL, blk)]
    side += [functools.partial(lambda n0: jnp.dot(attn, wa_ref[:, n0:n0 + blk], preferred_element_type=F32), n0)
             for n0 in range(0, D_MODEL, blk)]
    side_out = []

    tail = glu[ts - head:, :]
    cbuf[0:head, :] = _shift_in_groups(carry[...], tail)
    cbuf[head:head + ts, :] = glu
    carry[...] = tail
    sub_groups = CONV_ROWS // SUBLANES
    pace = jnp.zeros((SUBLANES, CONV_CH), F32)
    for i, r0 in enumerate(range(0, ts, CONV_ROWS)):
        acc = [dwb_ref[...] + pace] + [dwb_ref[...]] * (sub_groups - 1)
        wts = [dww_ref[k * SUBLANES:(k + 1) * SUBLANES, :] for k in range(CONV_WIDTH)]
        for m in range(CONV_WIDTH - 1 + sub_groups):
            grp = cbuf[r0 + m * SUBLANES:r0 + (m + 1) * SUBLANES, :]
            for a in range(sub_groups):
                if 0 <= m - a < CONV_WIDTH:
                    acc[a] = acc[a] + wts[m - a] * grp
        ybuf[r0:r0 + CONV_ROWS, :] = jnp.concatenate(acc, axis=0)
        if i < len(side):
            side_out.append(side[i]())
            pace = _dependent_zero(side_out[-1][ts - SUBLANES:, :], CONV_CH // blk)
    side_out += [f() for f in side[len(side_out):]]
    zm = jnp.concatenate(side_out[:2 * D_MODEL // blk], axis=1)
    y_a = jnp.concatenate(side_out[2 * D_MODEL // blk:], axis=1)

    y = _reorder(ybuf[...], pbuf, False)
    mu = jnp.mean(y, axis=-1, keepdims=True)
    var = jnp.mean(jnp.square(y - mu), axis=-1, keepdims=True)
    yn = (y - mu) * lax.rsqrt(var + EPS) * lng_ref[...] + lnb_ref[...]
    c = yn * jax.nn.sigmoid(yn)
    y_b = jnp.dot(c.astype(BF16), wb_ref[...], preferred_element_type=F32)
    mix = jax.nn.sigmoid(zm[:, :D_MODEL]) * y_a + jax.nn.sigmoid(zm[:, D_MODEL:]) * y_b
    h_ref[0] = x + jnp.dot(mix.astype(BF16), wo_ref[...], preferred_element_type=F32)


def _mix(x, g, w_all, attn, dww, dwb, lng, lnb, wa, wb, wo):
    b, seq, _ = x.shape
    ts = TS_MIX
    tile = lambda w: pl.BlockSpec((1, ts, w), lambda i, j: (i, j, 0))
    consts2 = [dww, dwb, lng, lnb, wa, wb, wo]
    gm_cols = w_all.shape[1] - (QKV_COLS + LANES)
    wgm_spec = pl.BlockSpec((pl.Element(D_MODEL), pl.Element(gm_cols)), lambda i, j: (0, QKV_COLS + LANES),
                            pipeline_mode=pl.Buffered(1))
    return pl.pallas_call(
        _mix_kernel,
        grid=(b, seq // ts),
        in_specs=[tile(D_MODEL), _const_spec(g.shape), wgm_spec]
                 + [tile(ATT_W)] + [_const_spec(a.shape) for a in consts2],
        out_specs=tile(D_MODEL),
        out_shape=jax.ShapeDtypeStruct((b, seq, D_MODEL), F32),
        scratch_shapes=[pltpu.VMEM(((CONV_WIDTH - 1) * SUBLANES + ts, CONV_CH), F32),
                        pltpu.VMEM(((CONV_WIDTH - 1) * SUBLANES, CONV_CH), F32),
                        pltpu.VMEM((ts, CONV_CH), F32),
                        pltpu.VMEM((_perm_rows(ts, CONV_CH), LANES), F32)],
        compiler_params=_cparams(("arbitrary", "arbitrary")),
        name="conv_merge",
    )(x, g, w_all, attn, dww, dwb, lng, lnb, wa, wb, wo)


def _ffn_kernel(h_ref, gf_ref, wup_ref, fw_ref, fb_ref, wdn_ref, p_ref, gp_ref, wpg_ref, wpp_ref,
                gfin_ref, out_ref, ubuf, carry, actbuf, hbuf, pbuf, *, final_norm):
    ts = h_ref.shape[1]
    fc = FFN_CHUNK
    n_chunks = FFN_DIM // fc
    head = (FFN_CONV_WIDTH - 1) * SUBLANES

    @pl.when(pl.program_id(1) == 0)
    def _():
        carry[...] = jnp.zeros_like(carry)

    h = _reorder(h_ref[0], hbuf, True)
    u = _rms(h, gf_ref[...]).astype(BF16)

    def conv_half(c, col0, slot):
        cols = slice(col0 + c * fc, col0 + (c + 1) * fc)
        up = jnp.dot(u, wup_ref[:, cols], preferred_element_type=F32)
        tail = up[ts - head:, :]
        buf = ubuf.at[slot]
        buf[0:head, :] = _shift_in_groups(carry[:, cols], tail)
        buf[head:head + ts, :] = up
        carry[:, cols] = tail
        conv = fb_ref[:, cols]
        for k in range(FFN_CONV_WIDTH):
            conv = conv + fw_ref[k:k + 1, cols] * buf[k * SUBLANES:k * SUBLANES + ts, :]
        return conv

    h2 = h
    for c0 in range(0, n_chunks, FFN_DOWN_GROUP):
        c1 = min(c0 + FFN_DOWN_GROUP, n_chunks)
        for c in range(c0, c1):
            a = conv_half(c, 0, (2 * c) % 4)
            gt = conv_half(c, FFN_DIM, (2 * c + 1) % 4)
            actbuf[:, c * fc:(c + 1) * fc] = (gt * jax.nn.sigmoid(gt) * a).astype(BF16)
        h2 = h2 + jnp.dot(actbuf[:, c0 * fc:c1 * fc], wdn_ref[c0 * fc:c1 * fc, :], preferred_element_type=F32)

    u3 = _rms(h2, gp_ref[...]).astype(BF16)
    gate = jax.nn.sigmoid(jnp.dot(u3, wpg_ref[...], preferred_element_type=F32))
    proj = jnp.dot(_reorder(p_ref[0], pbuf, True).astype(BF16), wpp_ref[...], preferred_element_type=F32)
    h3 = h2 + gate * proj
    out_ref[0] = _reorder(_rms(h3, gfin_ref[...]) if final_norm else h3, hbuf, False)


def _ffn(h, gf, wup, fw, fb, wdn, p, gp, wpg, wpp, gfin, final_norm):
    b, seq, _ = h.shape
    ts = TS_FFN
    tile = lambda w: pl.BlockSpec((1, ts, w), lambda i, j: (i, j, 0))
    single = lambda a: _const_spec(a.shape)
    return pl.pallas_call(
        functools.partial(_ffn_kernel, final_norm=final_norm),
        grid=(b, seq // ts),
        in_specs=[tile(D_MODEL), single(gf), single(wup), single(fw), single(fb), single(wdn),
                  tile(PLE_DIM), single(gp), single(wpg), single(wpp), single(gfin)],
        out_specs=tile(D_MODEL),
        out_shape=jax.ShapeDtypeStruct((b, seq, D_MODEL), F32),
        scratch_shapes=[pltpu.VMEM((4, (FFN_CONV_WIDTH - 1) * SUBLANES + ts, FFN_CHUNK), F32),
                        pltpu.VMEM(((FFN_CONV_WIDTH - 1) * SUBLANES, 2 * FFN_DIM), F32),
                        pltpu.VMEM((ts, FFN_DIM), BF16),
                        pltpu.VMEM((_perm_rows(ts, D_MODEL), LANES), F32),
                        pltpu.VMEM((_perm_rows(ts, PLE_DIM), LANES), F32)],
        compiler_params=_cparams(("arbitrary", "arbitrary")),
        name="conv_ffn",
    )(h, gf, wup, fw, fb, wdn, p, gp, wpg, wpp, gfin)


def _expand_cmp_weights(w1, w2):
    half = CMP_BLOCK // 2
    w1r = w1.reshape(2, half, HEAD_DIM, CMP_HIDDEN).astype(BF16)
    w2b = w2.astype(BF16)
    z1, z2 = jnp.zeros_like(w1r), jnp.zeros_like(w2b)
    wab = jnp.stack([jnp.concatenate([w1r, z1], axis=-1), jnp.concatenate([z1, w1r], axis=-1)], axis=2)
    wab = wab.reshape(2, half * KV_W, N_KV * CMP_HIDDEN)
    w2e = jnp.concatenate([jnp.concatenate([w2b, z2], axis=-1), jnp.concatenate([z2, w2b], axis=-1)], axis=0)
    return wab[0], wab[1], w2e


def _expand_pe(pe):
    half = CMP_BLOCK // 2
    per = jnp.broadcast_to(pe.reshape(2, half, 1, HEAD_DIM), (2, half, N_KV, HEAD_DIM))
    return per.reshape(2, half * KV_W)


def kernel(x, p, positions, norm_mix_g, w_in, pe_k, pe_v, cmp_k_w1, cmp_k_w2, cmp_v_w1, cmp_v_w2, conv_dw_w, conv_dw_b, conv_ln_g, conv_ln_b, w_a, w_b, w_o, norm_ffn_g, w_up, ffn_dw_w, ffn_dw_b, w_down, norm_ple_g, w_ple_gate, w_ple_proj, norm_final_g):
    b, seq, _ = x.shape
    depth = w_in.shape[0]
    n_tok = b * seq
    n_cmp = seq // CMP_STRIDE
    n_sel = seq // SEL_BLOCK
    assert seq % TS_MIX == 0 and seq % TK_SEL == 0 and seq >= WINDOW and n_sel <= HEAD_DIM
    assert FFN_DIM % FFN_CHUNK == 0 and seq % TS_FFN == 0 and N_KV == 2

    half = ROT_DIM // 2
    inv_col = (ROPE_THETA ** (-jnp.arange(0, ROT_DIM, 2, dtype=F32) / ROT_DIM)).reshape(half, 1)
    cmp_start = np.arange(n_cmp) * CMP_STRIDE
    sel_start = np.arange(n_sel) * SEL_BLOCK
    overlap = np.maximum(np.minimum(cmp_start[:, None] + CMP_BLOCK, sel_start[None, :] + SEL_BLOCK)
                         - np.maximum(cmp_start[:, None], sel_start[None, :]), 0).astype(np.float32) / CMP_STRIDE
    ov_t = jnp.asarray(np.pad(overlap.T, ((0, HEAD_DIM - n_sel), (0, 0))))

    cmp_end = np.minimum(np.arange(n_cmp) * CMP_STRIDE + CMP_BLOCK - 1, seq - 1)
    pos_all = jnp.concatenate([positions.reshape(-1), positions[:, cmp_end].reshape(-1)])
    pos_all = jnp.pad(pos_all, (0, (-pos_all.shape[0]) % ROPE_TB)).reshape(1, -1)
    tab = _rope_tables(pos_all, inv_col)
    d = np.arange(LANES) % HEAD_DIM
    cos_t = tab[np.where(d < ROT_DIM, d % half, 3 * half)].T
    sin_t = tab[np.where(d < half, half + d, np.where(d < ROT_DIM, half + d, 3 * half + 1))].T
    assert n_tok % n_cmp == 0

    h = x
    row = lambda v: v.reshape(1, -1)
    for i in range(depth):
        w_all = jnp.concatenate(
            [w_in[i][:, :QKV_COLS + GATE_COLS], jnp.zeros((D_MODEL, LANES - GATE_COLS), w_in.dtype),
             w_in[i][:, QKV_COLS + GATE_COLS:]], axis=1).astype(BF16)

        qpad, kc, vc, *seq_kv, gates = _inproj(
            h.reshape(n_tok, D_MODEL), row(norm_mix_g[i]), w_all, cos_t, sin_t, seq)

        wka, wkb, wk2 = _expand_cmp_weights(cmp_k_w1[i], cmp_k_w2[i])
        wva, wvb, wv2 = _expand_cmp_weights(cmp_v_w1[i], cmp_v_w2[i])
        kcmp, vcmp = _compress(kc.reshape(b, seq, KV_W), vc.reshape(b, seq, KV_W),
                               _expand_pe(pe_k[i]), _expand_pe(pe_v[i]),
                               wka, wkb, wk2, wva, wvb, wv2, cos_t, sin_t, n_tok)

        s3 = lambda a: a.reshape(b, seq, a.shape[-1])
        attn = _attention(s3(qpad), kcmp, vcmp, [s3(a) for a in seq_kv], s3(gates), ov_t)

        h = _mix(h, row(norm_mix_g[i]), w_all, attn,
                 jnp.repeat(conv_dw_w[i].reshape(CONV_WIDTH, CONV_CH), SUBLANES, axis=0),
                 jnp.broadcast_to(row(conv_dw_b[i]), (SUBLANES, CONV_CH)),
                 row(conv_ln_g[i]), row(conv_ln_b[i]),
                 w_a[i].astype(BF16), w_b[i].astype(BF16), w_o[i].astype(BF16))

        h = _ffn(h, row(norm_ffn_g[i]), w_up[i].astype(BF16),
                 ffn_dw_w[i].reshape(FFN_CONV_WIDTH, 2 * FFN_DIM), row(ffn_dw_b[i]),
                 w_down[i].astype(BF16), p[i], row(norm_ple_g[i]),
                 w_ple_gate[i].astype(BF16), w_ple_proj[i].astype(BF16), row(norm_final_g),
                 final_norm=(i == depth - 1))
    return h
```
